```python
import functools
import jax, jax.numpy as jnp
from jax import lax
import numpy as np

D_MODEL = 1024
BATCH = 16
SEQ = 2048
DEPTH = 2
DEC_BATCH = 128
DEC_SEQ = 4
PAST_LEN = 16384
PAGE_SIZE = 128

N_EVEN = (DEPTH + 1) // 2
N_ODD = DEPTH // 2
ALPHA = (2 * DEPTH) ** 0.25
BETA = (8 * DEPTH) ** -0.25
LN_EPS = 1e-5
A_HEADS = 8
A_HD = 64
A_W = A_HEADS * A_HD
LORA_W = 64
LORA_A = 64
LORA_G = 128
RWKV_SIZES = (A_W, A_W, A_W, LORA_W, LORA_A, LORA_G)
RWKV_COLS = sum(RWKV_SIZES)
GN_EPS = 64e-5
B_HEADS = 8
QK_NOPE = 64
QK_ROPE = 32
V_HD = 64
Q_LORA = 384
KV_LORA = 256
MLA_SIZES = (Q_LORA, KV_LORA, QK_ROPE)
MLA_COLS = sum(MLA_SIZES)
IN_E = RWKV_COLS + MLA_COLS
MIX_E = A_W + B_HEADS * V_HD
SM_SCALE = (QK_NOPE + QK_ROPE) ** -0.5
ROPE_BASE = 10000.0
Q_BLOCK = 128
C_HEADS = 8
C_DK = 64
C_DV = 128
MIX_O = C_HEADS * C_DV
MLSTM_SIZES = (C_HEADS * C_DK, C_HEADS * C_DK, MIX_O, MIX_O, C_HEADS, C_HEADS)
IN_O = sum(MLSTM_SIZES)
MLSTM_CHUNK = 64
GATE_CAP = 15.0
N_KEYS = 128
N_EXPERTS = N_KEYS * N_KEYS
PEER_HEADS = 8
PEER_QDIM = 256
PEER_TOPK = 16
PEER_BLOCK = 128

kernel_name = 'rwkv7_mla_mlstm_peer_deepnorm_step'


def split_last(x, sizes):
    idx = [int(s) for s in np.cumsum(sizes)[:-1]]
    return jnp.split(x, idx, axis=-1)


def layer_norm(x, g, b, eps=LN_EPS):
    x32 = x.astype(jnp.float32)
    mu = x32.mean(-1, keepdims=True)
    var = jnp.square(x32 - mu).mean(-1, keepdims=True)
    return ((x32 - mu) * lax.rsqrt(var + eps) * g.astype(jnp.float32) + b.astype(jnp.float32)).astype(x.dtype)


def rms_norm(x, g, eps=1e-6):
    x32 = x.astype(jnp.float32)
    return (x32 * lax.rsqrt(jnp.mean(x32 * x32, -1, keepdims=True) + eps) * g.astype(jnp.float32)).astype(x.dtype)


def soft_cap(x):
    return GATE_CAP * jnp.tanh(x / GATE_CAP)


def rope_tables(pos):
    inv = ROPE_BASE ** (-jnp.arange(0, QK_ROPE, 2, dtype=jnp.float32) / QK_ROPE)
    ang = pos.astype(jnp.float32)[:, None] * inv[None, :]
    return jnp.cos(ang), jnp.sin(ang)


def apply_rope(x, cos, sin):
    x1, x2 = jnp.split(x.astype(jnp.float32), 2, axis=-1)
    return jnp.concatenate([x1 * cos - x2 * sin, x1 * sin + x2 * cos], -1).astype(x.dtype)


def wkv7_scan(r, w, k, v, kk, a, s0):
    def step(s, inp):
        r_t, w_t, k_t, v_t, kk_t, a_t = inp
        sa = jnp.einsum('bhij,bhj->bhi', s, -kk_t)
        s = (s * w_t[:, :, None, :] + sa[..., None] * (kk_t * a_t)[:, :, None, :]
             + v_t[..., None] * k_t[:, :, None, :])
        return s, jnp.einsum('bhij,bhj->bhi', s, r_t)
    xs = tuple(jnp.moveaxis(t, 1, 0) for t in (r, w, k, v, kk, a))
    s_final, y = lax.scan(step, s0, xs)
    return jnp.moveaxis(y, 0, 1), s_final


def mla_attend_prompt(q_lat, q_pe, c, kpe):
    bsz, t_len, n_h, n_c = q_lat.shape
    nb = t_len // Q_BLOCK
    k_pos = jnp.arange(t_len)

    def blk(args):
        ql, qp, i = args
        s = (jnp.einsum('bqhc,bkc->bhqk', ql, c) + jnp.einsum('bqhr,bkr->bhqk', qp, kpe)).astype(jnp.float32) * SM_SCALE
        q_pos = i * Q_BLOCK + jnp.arange(Q_BLOCK)
        s = jnp.where(k_pos[None, :] <= q_pos[:, None], s, -jnp.inf)
        p = jax.nn.softmax(s, axis=-1).astype(c.dtype)
        return jnp.einsum('bhqk,bkc->bqhc', p, c)

    qb = lambda t: jnp.moveaxis(t.reshape(bsz, nb, Q_BLOCK, *t.shape[2:]), 1, 0)
    o = lax.map(blk, (qb(q_lat), qb(q_pe), jnp.arange(nb)))
    return jnp.moveaxis(o, 0, 1).reshape(bsz, t_len, n_h, n_c)


def mla_attend_paged(q_lat, q_pe, c, kpe, ckv_pool, kpe_pool, page_table):
    s_new_len = q_lat.shape[1]
    causal = jnp.tril(jnp.ones((s_new_len, s_new_len), bool))

    def one(args):
        pt, ql, qp, cn, kn = args
        cp = ckv_pool[pt].reshape(-1, KV_LORA)
        kp = kpe_pool[pt].reshape(-1, QK_ROPE)
        s_past = (jnp.einsum('qhc,kc->hqk', ql, cp) + jnp.einsum('qhr,kr->hqk', qp, kp)).astype(jnp.float32) * SM_SCALE
        s_new = (jnp.einsum('qhc,kc->hqk', ql, cn) + jnp.einsum('qhr,kr->hqk', qp, kn)).astype(jnp.float32) * SM_SCALE
        s_new = jnp.where(causal, s_new, -jnp.inf)
        p = jax.nn.softmax(jnp.concatenate([s_past, s_new], -1), axis=-1).astype(cp.dtype)
        n_past = cp.shape[0]
        return (jnp.einsum('hqk,kc->qhc', p[..., :n_past], cp)
                + jnp.einsum('hqk,kc->qhc', p[..., n_past:], cn))

    return lax.map(one, (page_table, q_lat, q_pe, c, kpe))


def rwkv_mla_mixer(x, x_prev, wkv0, pos, attend, w_in, mu, w0, w2, a0, a2, g2, k_k, k_a, r_k,
                   lnx_w, lnx_b, q_norm, w_uq, kv_norm, w_uk, w_uv, w_out):
    f32 = jnp.float32
    bsz, t_len, _ = x.shape
    heads = lambda t: t.reshape(bsz, t_len, A_HEADS, A_HD)
    proj = x @ w_in
    p_rw, p_mla = proj[..., :RWKV_COLS], proj[..., RWKV_COLS:]
    prev = jnp.concatenate([(x_prev @ w_in[:, :RWKV_COLS])[:, None], p_rw[:, :-1]], axis=1)
    p_rw = p_rw + (prev - p_rw) * mu
    r, k, v, lw, la, lg = split_last(p_rw, RWKV_SIZES)
    w_log = -jax.nn.softplus(-(w0 + jnp.tanh(lw) @ w2).astype(f32)) - 0.5
    decay = heads(jnp.exp(-jnp.exp(w_log)))
    a = heads(jax.nn.sigmoid((a0 + la @ a2).astype(f32)))
    g = jax.nn.sigmoid(lg) @ g2
    r, k, v = heads(r.astype(f32)), heads(k.astype(f32)), heads(v.astype(f32))
    kk = k * k_k.reshape(A_HEADS, A_HD).astype(f32)
    kk = kk / jnp.maximum(jnp.linalg.norm(kk, axis=-1, keepdims=True), 1e-12)
    k = k * (1.0 + (a - 1.0) * k_a.reshape(A_HEADS, A_HD).astype(f32))
    y, wkv_new = wkv7_scan(r, decay, k, v, kk, a, wkv0.astype(f32))
    y_mu = y.mean(-1, keepdims=True)
    y_var = jnp.square(y - y_mu).mean(-1, keepdims=True)
    y = ((y - y_mu) * lax.rsqrt(y_var + GN_EPS) * lnx_w.reshape(A_HEADS, A_HD).astype(f32)
         + lnx_b.reshape(A_HEADS, A_HD).astype(f32))
    y = y + jnp.sum(r * k * r_k.astype(f32), axis=-1, keepdims=True) * v
    out_a = (y.reshape(bsz, t_len, A_W) * g).astype(x.dtype)
    cq, ckv, kpe = split_last(p_mla, MLA_SIZES)
    q = (rms_norm(cq, q_norm) @ w_uq).reshape(bsz, t_len, B_HEADS, QK_NOPE + QK_ROPE)
    q_nope, q_pe = q[..., :QK_NOPE], q[..., QK_NOPE:]
    cos, sin = rope_tables(pos)
    q_pe = apply_rope(q_pe, cos[:, None], sin[:, None])
    c = rms_norm(ckv, kv_norm)
    kpe = apply_rope(kpe, cos, sin)
    q_lat = jnp.einsum('bthn,chn->bthc', q_nope, w_uk)
    o_lat = attend(q_lat, q_pe, c, kpe)
    out_b = jnp.einsum('bthc,chv->bthv', o_lat, w_uv).reshape(bsz, t_len, B_HEADS * V_HD)
    out = jnp.concatenate([out_a, out_b], axis=-1) @ w_out
    return out, c, kpe, x[:, -1], wkv_new.astype(x.dtype)


def mlstm_chunkwise(q, k, v, ig, lf, c0, n0, m0, chunk):
    f32 = jnp.float32
    bsz, t_len, n_h, _ = q.shape
    nc = t_len // chunk

    def to_chunks(t):
        t = t.astype(f32).reshape(bsz, nc, chunk, n_h, *t.shape[3:])
        return jnp.moveaxis(jnp.moveaxis(t, 1, 0), 3, 2)

    causal = jnp.tril(jnp.ones((chunk, chunk), bool))

    def step(carry, inp):
        c, n, m = carry
        qc, kc, vc, ic, fc = inp
        b = jnp.cumsum(fc, axis=-1)
        d = jnp.where(causal, b[..., :, None] - b[..., None, :] + ic[..., None, :], -jnp.inf)
        g = b + m[..., None]
        mt = jnp.maximum(g, d.max(-1))
        w_inter = jnp.exp(g - mt)
        att = jnp.exp(d - mt[..., None]) * jnp.einsum('bhtd,bhjd->bhtj', qc, kc)
        num = w_inter[..., None] * jnp.einsum('bhvd,bhtd->bhtv', c, qc) + jnp.einsum('bhtj,bhjv->bhtv', att, vc)
        den = w_inter * jnp.einsum('bhd,bhtd->bht', n, qc) + att.sum(-1)
        h = num / jnp.maximum(jnp.abs(den), jnp.exp(-mt))[..., None]
        m_new = mt[..., -1]
        w_state = jnp.exp(b[..., -1:] - b + ic - m_new[..., None])
        dec = jnp.exp(b[..., -1] + m - m_new)
        c = dec[..., None, None] * c + jnp.einsum('bhj,bhjv,bhjd->bhvd', w_state, vc, kc)
        n = dec[..., None] * n + jnp.einsum('bhj,bhjd->bhd', w_state, kc)
        return (c, n, m_new), h

    (c, n, m), h = lax.scan(step, (c0.astype(f32), n0.astype(f32), m0.astype(f32)),
                            tuple(to_chunks(t) for t in (q, k, v, ig, lf)))
    h = jnp.moveaxis(jnp.moveaxis(h, 2, 3), 0, 1).reshape(bsz, t_len, n_h, -1)
    return h, c, n, m


def mlstm_mixer(x, c0, n0, m0, w_in, b_i, b_f, mh_norm, w_out):
    f32 = jnp.float32
    bsz, t_len, _ = x.shape
    q, k, v, o, i_pre, f_pre = split_last(x @ w_in, MLSTM_SIZES)
    q = q.reshape(bsz, t_len, C_HEADS, C_DK) * (C_DK ** -0.5)
    k = k.reshape(bsz, t_len, C_HEADS, C_DK)
    v = v.reshape(bsz, t_len, C_HEADS, C_DV)
    ig = soft_cap((i_pre + b_i).astype(f32))
    lf = jax.nn.log_sigmoid(soft_cap((f_pre + b_f).astype(f32)))
    chunk = MLSTM_CHUNK if t_len % MLSTM_CHUNK == 0 else t_len
    h, c_new, n_new, m_new = mlstm_chunkwise(q, k, v, ig, lf, c0, n0, m0, chunk)
    h = rms_norm(h.astype(x.dtype), mh_norm.reshape(C_HEADS, C_DV)).reshape(bsz, t_len, MIX_O)
    out = (h * jax.nn.sigmoid(o)) @ w_out
    return out, c_new.astype(x.dtype), n_new.astype(x.dtype), m_new.astype(x.dtype)


def peer(x, wq, k1, k2, u_tab, v_tab):
    shp = x.shape
    xt = x.reshape(-1, D_MODEL)
    n_tok = xt.shape[0]
    nb = -(-n_tok // PEER_BLOCK)
    xt = jnp.pad(xt, ((0, nb * PEER_BLOCK - n_tok), (0, 0))).reshape(nb, PEER_BLOCK, D_MODEL)
    half = PEER_QDIM // 2

    def blk(xb):
        q = (xb @ wq).reshape(PEER_BLOCK, PEER_HEADS, PEER_QDIM).astype(jnp.float32)
        s1 = jnp.einsum('bhd,nd->bhn', q[..., :half], k1.astype(jnp.float32))
        s2 = jnp.einsum('bhd,nd->bhn', q[..., half:], k2.astype(jnp.float32))
        t1, i1 = lax.top_k(s1, PEER_TOPK)
        t2, i2 = lax.top_k(s2, PEER_TOPK)
        cand = (t1[..., :, None] + t2[..., None, :]).reshape(PEER_BLOCK, PEER_HEADS, PEER_TOPK * PEER_TOPK)
        cidx = (i1[..., :, None] * N_KEYS + i2[..., None, :]).reshape(PEER_BLOCK, PEER_HEADS, PEER_TOPK * PEER_TOPK)
        top, sel = lax.top_k(cand, PEER_TOPK)
        idx = jnp.take_along_axis(cidx, sel, axis=-1)
        gate = jax.nn.softmax(top, axis=-1).astype(xb.dtype)
        act = jax.nn.gelu(jnp.einsum('bd,bhkd->bhk', xb, u_tab[idx]), approximate=False) * gate
        return jnp.einsum('bhk,bhkd->bd', act, v_tab[idx])

    y = lax.map(blk, xt).reshape(-1, D_MODEL)[:n_tok]
    return y.reshape(shp)


def setup_inputs(seed: int = 0) -> dict:
    key = jax.random.key(seed)
    ks = iter(jax.random.split(key, 64))
    f32 = jnp.float32

    def nrm(shape, scale=1.0):
        return jax.random.normal(next(ks), shape, f32) * scale

    def gain(shape):
        return 1.0 + nrm(shape, 0.02)

    n_pages = PAST_LEN // PAGE_SIZE
    n_used = DEC_BATCH * n_pages
    n_pool = n_used + max(1, n_used // 4)
    page_table = jax.random.permutation(next(ks), n_pool)[:n_used].reshape(DEC_BATCH, n_pages).astype(jnp.int32)
    d_in = D_MODEL ** -0.5
    return {
        'x_prompt': nrm((BATCH, SEQ, D_MODEL)),
        'x_sample': nrm((DEC_BATCH, DEC_SEQ, D_MODEL)),
        'cache_ckv': nrm((N_EVEN, n_pool, PAGE_SIZE, KV_LORA)),
        'cache_kpe': nrm((N_EVEN, n_pool, PAGE_SIZE, QK_ROPE)),
        'page_table': page_table,
        'state_shift': nrm((N_EVEN, DEC_BATCH, D_MODEL)),
        'state_wkv': nrm((N_EVEN, DEC_BATCH, A_HEADS, A_HD, A_HD), 0.3),
        'state_mlstm_c': nrm((N_ODD, DEC_BATCH, C_HEADS, C_DV, C_DK), 0.3),
        'state_mlstm_n': nrm((N_ODD, DEC_BATCH, C_HEADS, C_DK), 0.3),
        'state_mlstm_m': nrm((N_ODD, DEC_BATCH, C_HEADS), 0.5),
        'w_in_e': nrm((N_EVEN, D_MODEL, IN_E), d_in),
        'mu_e': jax.random.uniform(next(ks), (N_EVEN, RWKV_COLS), f32),
        'w0_e': jax.random.uniform(next(ks), (N_EVEN, A_W), f32, -6.0, 1.0),
        'w2_e': nrm((N_EVEN, LORA_W, A_W), 0.1 * LORA_W ** -0.5),
        'a0_e': nrm((N_EVEN, A_W), 0.5),
        'a2_e': nrm((N_EVEN, LORA_A, A_W), 0.5 * LORA_A ** -0.5),
        'g2_e': nrm((N_EVEN, LORA_G, A_W), LORA_G ** -0.5),
        'kk_e': 0.85 + nrm((N_EVEN, A_W), 0.05),
        'ka_e': 1.0 + nrm((N_EVEN, A_W), 0.05),
        'rk_e': nrm((N_EVEN, A_HEADS, A_HD), 0.1),
        'lnx_w_e': gain((N_EVEN, A_W)),
        'lnx_b_e': nrm((N_EVEN, A_W), 0.02),
        'q_norm_e': gain((N_EVEN, Q_LORA)),
        'w_uq_e': nrm((N_EVEN, Q_LORA, B_HEADS * (QK_NOPE + QK_ROPE)), Q_LORA ** -0.5),
        'kv_norm_e': gain((N_EVEN, KV_LORA)),
        'w_uk_e': nrm((N_EVEN, KV_LORA, B_HEADS, QK_NOPE), KV_LORA ** -0.5),
        'w_uv_e': nrm((N_EVEN, KV_LORA, B_HEADS, V_HD), KV_LORA ** -0.5),
        'w_out_e': nrm((N_EVEN, MIX_E, D_MODEL), BETA * MIX_E ** -0.5),
        'w_in_o': nrm((N_ODD, D_MODEL, IN_O), d_in),
        'b_i_o': nrm((N_ODD, C_HEADS), 0.5),
        'b_f_o': jnp.linspace(3.0, 6.0, C_HEADS, dtype=f32) + nrm((N_ODD, C_HEADS), 0.1),
        'mh_norm_o': gain((N_ODD, MIX_O)),
        'w_out_o': nrm((N_ODD, MIX_O, D_MODEL), BETA * MIX_O ** -0.5),
        'peer_wq': nrm((DEPTH, D_MODEL, PEER_HEADS * PEER_QDIM), d_in),
        'peer_k1': nrm((DEPTH, N_KEYS, PEER_QDIM // 2), (PEER_QDIM // 2) ** -0.5),
        'peer_k2': nrm((DEPTH, N_KEYS, PEER_QDIM // 2), (PEER_QDIM // 2) ** -0.5),
        'peer_u': nrm((DEPTH, N_EXPERTS, D_MODEL), d_in),
        'peer_v': nrm((DEPTH, N_EXPERTS, D_MODEL), BETA * 0.5),
        'ln1_g': gain((DEPTH, D_MODEL)),
        'ln1_b': nrm((DEPTH, D_MODEL), 0.02),
        'ln2_g': gain((DEPTH, D_MODEL)),
        'ln2_b': nrm((DEPTH, D_MODEL), 0.02),
    }


def reference(x_prompt, x_sample, cache_ckv, cache_kpe, page_table, state_shift, state_wkv,
              state_mlstm_c, state_mlstm_n, state_mlstm_m,
              w_in_e, mu_e, w0_e, w2_e, a0_e, a2_e, g2_e, kk_e, ka_e, rk_e, lnx_w_e, lnx_b_e,
              q_norm_e, w_uq_e, kv_norm_e, w_uk_e, w_uv_e, w_out_e,
              w_in_o, b_i_o, b_f_o, mh_norm_o, w_out_o,
              peer_wq, peer_k1, peer_k2, peer_u, peer_v,
              ln1_g, ln1_b, ln2_g, ln2_b):
    xp, xs = x_prompt, x_sample
    bp = xp.shape[0]
    past_len = page_table.shape[1] * PAGE_SIZE
    pos_p = jnp.arange(xp.shape[1])
    pos_s = past_len + jnp.arange(xs.shape[1])
    ckv_p, kpe_p, ckv_s, kpe_s, sh_p, sh_s, wkv_p, wkv_s = ([] for _ in range(8))
    mc_p, mc_s, mn_p, mn_s, mm_p, mm_s = ([] for _ in range(6))
    for l in range(DEPTH):
        if l % 2 == 0:
            e = l // 2
            wts = (w_in_e[e], mu_e[e], w0_e[e], w2_e[e], a0_e[e], a2_e[e], g2_e[e], kk_e[e], ka_e[e],
                   rk_e[e], lnx_w_e[e], lnx_b_e[e], q_norm_e[e], w_uq_e[e], kv_norm_e[e], w_uk_e[e],
                   w_uv_e[e], w_out_e[e])
            hp, c1, k1, last1, s1 = rwkv_mla_mixer(
                xp, jnp.zeros((bp, D_MODEL), xp.dtype), jnp.zeros((bp, A_HEADS, A_HD, A_HD), xp.dtype),
                pos_p, mla_attend_prompt, *wts)
            attend_s = functools.partial(mla_attend_paged, ckv_pool=cache_ckv[e], kpe_pool=cache_kpe[e],
                                         page_table=page_table)
            hs, c2, k2, last2, s2 = rwkv_mla_mixer(xs, state_shift[e], state_wkv[e], pos_s, attend_s, *wts)
            ckv_p.append(c1); kpe_p.append(k1); ckv_s.append(c2); kpe_s.append(k2)
            sh_p.append(last1); sh_s.append(last2); wkv_p.append(s1); wkv_s.append(s2)
        else:
            od = l // 2
            wts = (w_in_o[od], b_i_o[od], b_f_o[od], mh_norm_o[od], w_out_o[od])
            hp, cp1, np1, mp1 = mlstm_mixer(
                xp, jnp.zeros((bp, C_HEADS, C_DV, C_DK), xp.dtype), jnp.zeros((bp, C_HEADS, C_DK), xp.dtype),
                jnp.zeros((bp, C_HEADS), xp.dtype), *wts)
            hs, cs1, ns1, ms1 = mlstm_mixer(xs, state_mlstm_c[od], state_mlstm_n[od], state_mlstm_m[od], *wts)
            mc_p.append(cp1); mn_p.append(np1); mm_p.append(mp1)
            mc_s.append(cs1); mn_s.append(ns1); mm_s.append(ms1)
        xp = layer_norm(ALPHA * xp + hp, ln1_g[l], ln1_b[l])
        xs = layer_norm(ALPHA * xs + hs, ln1_g[l], ln1_b[l])
        pw = (peer_wq[l], peer_k1[l], peer_k2[l], peer_u[l], peer_v[l])
        xp = layer_norm(ALPHA * xp + peer(xp, *pw), ln2_g[l], ln2_b[l])
        xs = layer_norm(ALPHA * xs + peer(xs, *pw), ln2_g[l], ln2_b[l])
    y_prompt, y_sample = xp, xs
    new_ckv_prompt, new_kpe_prompt = jnp.stack(ckv_p), jnp.stack(kpe_p)
    new_ckv_sample, new_kpe_sample = jnp.stack(ckv_s), jnp.stack(kpe_s)
    new_shift_prompt, new_shift_sample = jnp.stack(sh_p), jnp.stack(sh_s)
    new_wkv_prompt, new_wkv_sample = jnp.stack(wkv_p), jnp.stack(wkv_s)
    new_c_prompt, new_c_sample = jnp.stack(mc_p), jnp.stack(mc_s)
    new_n_prompt, new_n_sample = jnp.stack(mn_p), jnp.stack(mn_s)
    new_m_prompt, new_m_sample = jnp.stack(mm_p), jnp.stack(mm_s)
    return (y_prompt, y_sample, new_ckv_prompt, new_kpe_prompt, new_ckv_sample, new_kpe_sample,
            new_shift_prompt, new_shift_sample, new_wkv_prompt, new_wkv_sample,
            new_c_prompt, new_c_sample, new_n_prompt, new_n_sample, new_m_prompt, new_m_sample)
```

```python
import functools

import jax
import jax.numpy as jnp
import numpy as np
from jax import lax
from jax.experimental import pallas as pl
from jax.experimental.pallas import tpu as pltpu

F32 = jnp.float32
BF16 = jnp.bfloat16
NEG_INF = float("-inf")

LANES = 128
SUBLANES = 8
VMEM_LIMIT_BYTES = 56 * 1024 * 1024

D_MODEL = 1024
DEPTH = 2
PAGE_SIZE = 128
ALPHA = (2 * DEPTH) ** 0.25
LN_EPS = 1e-5
A_HEADS, A_HD = 8, 64
A_W = A_HEADS * A_HD
LORA_W, LORA_A, LORA_G = 64, 64, 128
RWKV_COLS = 3 * A_W + LORA_W + LORA_A + LORA_G
GN_EPS = 64e-5
B_HEADS, QK_NOPE, QK_ROPE, V_HD = 8, 64, 32, 64
Q_LORA, KV_LORA = 384, 256
SM_SCALE = (QK_NOPE + QK_ROPE) ** -0.5
ROPE_BASE = 10000.0
ROPE_HALF = QK_ROPE // 2
C_HEADS, C_DK, C_DV = 8, 64, 128
MIX_O = C_HEADS * C_DV
MLSTM_CHUNK = 64
GATE_CAP = 15.0
GATE_PAD = -1e30
N_KEYS = 128
N_EXPERTS = N_KEYS * N_KEYS
PEER_HEADS, PEER_QDIM, PEER_TOPK = 8, 256, 16
PEER_HALF = PEER_QDIM // 2
PEER_PAIRS = PEER_HEADS * PEER_TOPK

RW_PAD = 3 * A_W + 3 * LANES
OFF_CQ = RW_PAD
OFF_CKV = OFF_CQ + Q_LORA
OFF_KP1 = OFF_CKV + KV_LORA
OFF_KP2 = OFF_KP1 + LANES
IN_E_PAD = OFF_KP2 + LANES
Q_PAD = B_HEADS * LANES + 2 * LANES
QCAT = 2 * KV_LORA
HEAD_PAD = LANES
OFF_MK = C_HEADS * HEAD_PAD
OFF_MV = 2 * C_HEADS * HEAD_PAD
OFF_MO = OFF_MV + MIX_O
OFF_MG = OFF_MO + MIX_O
IN_O_PAD = OFF_MG + LANES


def _params(*sem):
    return pltpu.CompilerParams(dimension_semantics=sem, vmem_limit_bytes=VMEM_LIMIT_BYTES)


def _dot(a, b):
    return jnp.dot(a, b, preferred_element_type=F32)


def _dot_nt(a, b):
    return lax.dot_general(a, b, (((1,), (1,)), ((), ())), preferred_element_type=F32)


def _dot_tn(a, b):
    return lax.dot_general(a, b, (((0,), (0,)), ((), ())), preferred_element_type=F32)


def _seg_sum(x, seg):
    hi = x.astype(BF16)
    lo = (x - hi.astype(F32)).astype(BF16)
    return _dot(hi, seg) + _dot(lo, seg)


def _layer_norm_res(x, h, g, b):
    z = ALPHA * x + h
    mu = jnp.mean(z, axis=-1, keepdims=True)
    zc = z - mu
    var = jnp.mean(zc * zc, axis=-1, keepdims=True)
    return zc * lax.rsqrt(var + LN_EPS) * g + b


def _full(shape):
    nd = len(shape)
    return pl.BlockSpec(shape, lambda *_: (0,) * nd)


def _mm_kernel(x_ref, w_ref, o_ref):
    o_ref[...] = _dot(x_ref[...].astype(BF16), w_ref[...])


def _mm(x, w, tm):
    m, k = x.shape
    n = w.shape[1]
    return pl.pallas_call(
        _mm_kernel,
        grid=(m // tm,),
        in_specs=[pl.BlockSpec((tm, k), lambda i: (i, 0)), _full((k, n))],
        out_specs=pl.BlockSpec((tm, n), lambda i: (i, 0)),
        out_shape=jax.ShapeDtypeStruct((m, n), F32),
        compiler_params=_params("parallel"),
        name="mm",
    )(x, w)


def _ka_kernel(has_start, seq_len, tm, *refs):
    if has_start:
        x_ref, start_ref = refs[:2]
        refs = refs[2:]
    else:
        x_ref = refs[0]
        start_ref = None
        refs = refs[1:]
    (w_in_ref, mu_ref, w0_ref, w2_ref, a0_ref, a2_ref, g2_ref, kk_ref, ka_ref, rk_ref, qn_ref,
     wuq_ref, kvn_ref, wuk_ref, cos_ref, sin_ref, seg_ref, pem_ref,
     r_out, w_out, k_out, v_out, kk_out, kka_out, g_out, bonus_out, qcat_out, qpe_out, c_out,
     kpe_out, kcat_out, carry_ref) = refs
    t_blk = pl.program_id(1)
    proj = _dot(x_ref[...].astype(BF16), w_in_ref[...])
    p_rw = proj[:, :RW_PAD]
    row = lax.broadcasted_iota(jnp.int32, (tm, 1), 0)
    prev = pltpu.roll(p_rw, 1, 0)
    if has_start:
        prev = jnp.where(row % seq_len == 0, start_ref[...], prev)
    else:
        @pl.when(t_blk == 0)
        def _():
            carry_ref[...] = jnp.zeros_like(carry_ref)

        prev = jnp.where(row == 0, carry_ref[SUBLANES - 1:SUBLANES, :], prev)
        carry_ref[...] = p_rw[tm - SUBLANES:, :]
    p = p_rw + (prev - p_rw) * mu_ref[...]
    r = p[:, 0:A_W]
    k = p[:, A_W:2 * A_W]
    v = p[:, 2 * A_W:3 * A_W]
    lw = p[:, 3 * A_W:3 * A_W + LANES]
    la = p[:, 3 * A_W + LANES:3 * A_W + 2 * LANES]
    lg = p[:, 3 * A_W + 2 * LANES:RW_PAD]
    w_pre = w0_ref[...] + _dot(jnp.tanh(lw).astype(BF16), w2_ref[...])
    w_log = -jax.nn.softplus(-w_pre) - 0.5
    decay = jnp.exp(-jnp.exp(w_log))
    a = jax.nn.sigmoid(a0_ref[...] + _dot(la.astype(BF16), a2_ref[...]))
    g = _dot(jax.nn.sigmoid(lg).astype(BF16), g2_ref[...])
    seg = seg_ref[...]
    kk = k * kk_ref[...]
    kk = kk / jnp.maximum(jnp.sqrt(_seg_sum(kk * kk, seg)), 1e-12)
    k = k * (1.0 + (a - 1.0) * ka_ref[...])
    r_out[...] = r
    w_out[...] = decay
    k_out[...] = k
    v_out[...] = v
    kk_out[...] = kk
    kka_out[...] = kk * a
    g_out[...] = g
    bonus_out[...] = _seg_sum(r * k * rk_ref[...], seg) * v
    cq = proj[:, OFF_CQ:OFF_CKV]
    ckv = proj[:, OFF_CKV:OFF_KP1]
    kp1 = proj[:, OFF_KP1:OFF_KP2]
    kp2 = proj[:, OFF_KP2:IN_E_PAD]
    cos = cos_ref[...]
    sin = sin_ref[...]
    cqn = cq * lax.rsqrt(jnp.mean(cq * cq, axis=-1, keepdims=True) + 1e-6) * qn_ref[...]
    q = _dot(cqn.astype(BF16), wuq_ref[...])
    x1 = q[:, B_HEADS * LANES:B_HEADS * LANES + LANES]
    x2 = q[:, B_HEADS * LANES + LANES:Q_PAD]
    qpe = jnp.concatenate([x1 * cos - x2 * sin, x1 * sin + x2 * cos], axis=-1)
    qpe_out[...] = qpe
    for h in range(B_HEADS):
        q_lat = _dot(q[:, h * LANES:(h + 1) * LANES].astype(BF16), wuk_ref[h])
        qcat_out[:, h * QCAT:h * QCAT + KV_LORA] = q_lat.astype(BF16)
        qcat_out[:, h * QCAT + KV_LORA:(h + 1) * QCAT] = (qpe * pem_ref[h:h + 1, :]).astype(BF16)
    c = ckv * lax.rsqrt(jnp.mean(ckv * ckv, axis=-1, keepdims=True) + 1e-6) * kvn_ref[...]
    kpe = jnp.concatenate([kp1 * cos - kp2 * sin, kp1 * sin + kp2 * cos], axis=-1)
    c_out[...] = c
    kpe_out[...] = kpe
    kcat_out[:, :KV_LORA] = c.astype(BF16)
    kcat_out[:, KV_LORA:] = kpe.astype(BF16)


def _layer0_prep(x, start, wts, cos, sin, n_seq_blocks, n_t_blocks, tm, seq_len):
    rows = x.shape[0]
    has_start = start is not None
    rmap = lambda b, t: (b * n_t_blocks + t, 0)
    tmap = lambda b, t: (t, 0)
    ins = [x] + ([start] if has_start else [])
    in_specs = [pl.BlockSpec((tm, D_MODEL), rmap)]
    if has_start:
        in_specs.append(pl.BlockSpec((tm, RW_PAD), rmap))
    w_names = ("w_in", "mu", "w0", "w2", "a0", "a2", "g2", "kk", "ka", "rk", "q_norm", "w_uq",
               "kv_norm", "w_uk")
    for nme in w_names:
        ins.append(wts[nme])
        in_specs.append(_full(wts[nme].shape))
    ins += [cos, sin, wts["seg"], wts["pe_mask"]]
    in_specs += [pl.BlockSpec((tm, LANES), tmap), pl.BlockSpec((tm, LANES), tmap),
                 _full(wts["seg"].shape), _full(wts["pe_mask"].shape)]
    widths = [A_W] * 8 + [B_HEADS * QCAT, 2 * LANES, KV_LORA, 2 * LANES, QCAT]
    dtypes = [F32] * 8 + [BF16, F32, F32, F32, BF16]
    out_shape = [jax.ShapeDtypeStruct((rows, wd), dt) for wd, dt in zip(widths, dtypes)]
    out_specs = [pl.BlockSpec((tm, wd), rmap) for wd in widths]
    return pl.pallas_call(
        functools.partial(_ka_kernel, has_start, seq_len, tm),
        grid=(n_seq_blocks, n_t_blocks),
        in_specs=in_specs,
        out_specs=out_specs,
        out_shape=out_shape,
        scratch_shapes=[pltpu.VMEM((SUBLANES, RW_PAD), F32)],
        compiler_params=_params("parallel", "arbitrary"),
        name="layer0_prep",
    )(*ins)


def _wkv_kernel(tch, r_ref, w_ref, k_ref, v_ref, kk_ref, kka_ref, s0_ref, y_ref, sout_ref,
                s_ref, sa_ref):
    tc = pl.program_id(1)

    @pl.when(tc == 0)
    def _():
        s_ref[...] = s0_ref[...]

    acc = jnp.zeros((A_HD, LANES), F32)
    for j in range(A_HD):
        acc = acc + s_ref[j] * kk_ref[0, pl.ds(j, 1), :]
    sa_ref[...] = -acc

    def step(t, carry):
        tn = jnp.minimum(t + 1, tch - 1)
        sa = sa_ref[...]
        v_t = v_ref[t]
        y = jnp.zeros((A_HD, LANES), F32)
        san = jnp.zeros((A_HD, LANES), F32)
        for j in range(A_HD):
            row = pl.ds(j, 1)
            sj = s_ref[j] * w_ref[t, row, :] + sa * kka_ref[t, row, :] + v_t * k_ref[t, row, :]
            s_ref[j] = sj
            y = y + sj * r_ref[t, row, :]
            san = san + sj * kk_ref[tn, row, :]
        y_ref[t] = y
        sa_ref[...] = -san
        return carry

    lax.fori_loop(0, tch, step, 0)

    @pl.when(tc == pl.num_programs(1) - 1)
    def _():
        sout_ref[...] = s_ref[...]


def _wkv_scan(r, w, k, v, kk, kka, s0, tch):
    t_len, _, n_chains = r.shape
    pad = (-n_chains) % LANES
    if pad:
        padc = lambda a: jnp.pad(a, ((0, 0), (0, 0), (0, pad)))
        y, s = _wkv_scan(*(padc(a) for a in (r, w, k, v, kk, kka, s0)), tch)
        return y[..., :n_chains], s[..., :n_chains]
    chains = n_chains
    seq = pl.BlockSpec((tch, A_HD, LANES), lambda c, t: (t, 0, c))
    st = pl.BlockSpec((A_HD, A_HD, LANES), lambda c, t: (0, 0, c))
    return pl.pallas_call(
        functools.partial(_wkv_kernel, tch),
        grid=(chains // LANES, t_len // tch),
        in_specs=[seq] * 6 + [st],
        out_specs=[seq, st],
        out_shape=[jax.ShapeDtypeStruct((t_len, A_HD, chains), F32),
                   jax.ShapeDtypeStruct((A_HD, A_HD, chains), F32)],
        scratch_shapes=[pltpu.VMEM((A_HD, A_HD, LANES), F32), pltpu.VMEM((A_HD, LANES), F32)],
        compiler_params=_params("parallel", "arbitrary"),
        name="wkv_scan",
    )(r, w, k, v, kk, kka, s0)


def _to_chains(x, bsz, t_len):
    x = x.reshape(bsz, t_len, A_HEADS, A_HD)
    return jnp.transpose(x, (1, 3, 0, 2)).reshape(t_len, A_HD, bsz * A_HEADS)


def _from_chains(y, bsz, t_len):
    y = y.reshape(t_len, A_HD, bsz, A_HEADS)
    return jnp.transpose(y, (2, 0, 3, 1)).reshape(bsz * t_len, A_W)


def _kb_kernel(y_ref, g_ref, bonus_ref, ob_ref, x_ref, lw_ref, lb_ref, seg_ref, wo_ref, ng_ref,
               nb_ref, o_ref):
    seg = seg_ref[...]
    y = y_ref[...]
    mu = _seg_sum(y, seg) * (1.0 / A_HD)
    yc = y - mu
    var = _seg_sum(yc * yc, seg) * (1.0 / A_HD)
    y = yc * lax.rsqrt(var + GN_EPS) * lw_ref[...] + lb_ref[...] + bonus_ref[...]
    out_a = (y * g_ref[...]).astype(BF16)
    h = _dot(out_a, wo_ref[:A_W, :]) + _dot(ob_ref[...].astype(BF16), wo_ref[A_W:, :])
    o_ref[...] = _layer_norm_res(x_ref[...], h, ng_ref[...], nb_ref[...])


def _layer0_out(y, g, bonus, out_b, x, wts, ln_g, ln_b, tm):
    rows = x.shape[0]
    half = pl.BlockSpec((tm, A_W), lambda i: (i, 0))
    full = pl.BlockSpec((tm, D_MODEL), lambda i: (i, 0))
    consts = [wts["lnx_w"], wts["lnx_b"], wts["seg"], wts["w_out"], ln_g, ln_b]
    return pl.pallas_call(
        _kb_kernel,
        grid=(rows // tm,),
        in_specs=[half, half, half, half, full] + [_full(c.shape) for c in consts],
        out_specs=full,
        out_shape=jax.ShapeDtypeStruct((rows, D_MODEL), F32),
        compiler_params=_params("parallel"),
        name="layer0_out",
    )(y, g, bonus, out_b, x, *consts)


def _mla_p_kernel(tq, q_ref, kc_ref, wuv_ref, o_ref):
    qi = pl.program_id(1)
    rr = lax.broadcasted_iota(jnp.int32, (tq, tq), 0)
    cc = lax.broadcasted_iota(jnp.int32, (tq, tq), 1)
    causal = cc <= rr
    out = jnp.zeros((tq, B_HEADS * V_HD), F32)
    for h in range(B_HEADS):
        qh = q_ref[:, h * QCAT:(h + 1) * QCAT]

        def update(kc, carry, mask):
            m, l, acc = carry
            s = _dot_nt(qh, kc) * SM_SCALE
            if mask:
                s = jnp.where(causal, s, NEG_INF)
            m_new = jnp.maximum(m, jnp.max(s, axis=-1, keepdims=True))
            alpha = jnp.exp(m - m_new)
            p = jnp.exp(s - m_new)
            l = alpha * l + jnp.sum(p, axis=-1, keepdims=True)
            acc = alpha * acc + _dot(p.astype(BF16), kc[:, :KV_LORA])
            return m_new, l, acc

        def body(c, carry):
            kc = kc_ref[pl.ds(pl.multiple_of(c * tq, tq), tq), :]
            return update(kc, carry, False)

        init = (jnp.full((tq, 1), NEG_INF, F32), jnp.zeros((tq, 1), F32),
                jnp.zeros((tq, KV_LORA), F32))
        carry = lax.fori_loop(0, qi, body, init)
        kc = kc_ref[pl.ds(pl.multiple_of(qi * tq, tq), tq), :]
        _, l, acc = update(kc, carry, True)
        out = out + _dot((acc / l).astype(BF16), wuv_ref[h])
    o_ref[...] = out


def _mla_prompt(qcat, kcat, wuv_pad, bsz, t_len, tq):
    nq = t_len // tq
    return pl.pallas_call(
        functools.partial(_mla_p_kernel, tq),
        grid=(bsz, nq),
        in_specs=[pl.BlockSpec((tq, B_HEADS * QCAT), lambda b, i: (b * nq + i, 0)),
                  pl.BlockSpec((t_len, QCAT), lambda b, i: (b, 0)),
                  _full(wuv_pad.shape)],
        out_specs=pl.BlockSpec((tq, B_HEADS * V_HD), lambda b, i: (b * nq + i, 0)),
        out_shape=jax.ShapeDtypeStruct((bsz * t_len, B_HEADS * V_HD), F32),
        compiler_params=_params("parallel", "arbitrary"),
        name="mla_prompt",
    )(qcat, kcat, wuv_pad)


PAGES_PER_STEP = 16


def _mla_s_kernel(n_new, pt_ref, ql_ref, qp_ref, cn_ref, kn_ref, *rest):
    ck_refs = rest[:PAGES_PER_STEP]
    kp_refs = rest[PAGES_PER_STEP:2 * PAGES_PER_STEP]
    wuv_ref, hm_ref, o_ref, m_ref, l_ref, acc_ref = rest[2 * PAGES_PER_STEP:]
    step = pl.program_id(1)
    ql = ql_ref[...]
    qp = qp_ref[...]
    n_rows = ql.shape[0]

    @pl.when(step == 0)
    def _():
        cn = cn_ref[...].astype(BF16)
        kn = kn_ref[...].astype(BF16)
        s = (_dot_nt(ql, cn) + _dot_nt(qp, kn)) * SM_SCALE
        tok = lax.broadcasted_iota(jnp.int32, s.shape, 0) // B_HEADS
        col = lax.broadcasted_iota(jnp.int32, s.shape, 1)
        s = jnp.where(col <= tok, s, NEG_INF)
        m = jnp.max(s, axis=-1, keepdims=True)
        p = jnp.exp(s - m)
        m_ref[...] = m
        l_ref[...] = jnp.sum(p, axis=-1, keepdims=True)
        acc_ref[...] = _dot(p.astype(BF16), cn)

    cb = jnp.concatenate([r[...].astype(BF16) for r in ck_refs], axis=0)
    kb = jnp.concatenate([r[...].astype(BF16) for r in kp_refs], axis=0)
    s = (_dot_nt(ql, cb) + _dot_nt(qp, kb)) * SM_SCALE
    m = m_ref[...]
    m_new = jnp.maximum(m, jnp.max(s, axis=-1, keepdims=True))
    alpha = jnp.exp(m - m_new)
    p = jnp.exp(s - m_new)
    m_ref[...] = m_new
    l_ref[...] = alpha * l_ref[...] + jnp.sum(p, axis=-1, keepdims=True)
    acc_ref[...] = alpha * acc_ref[...] + _dot(p.astype(BF16), cb)

    @pl.when(step == pl.num_programs(1) - 1)
    def _():
        o = (acc_ref[...] / l_ref[...]).astype(BF16)
        full = _dot(o, wuv_ref[...])
        full = full.reshape(n_new, B_HEADS, B_HEADS * V_HD) * hm_ref[...]
        o_ref[...] = jnp.sum(full, axis=1)


def _mla_sample(q_lat, q_pe, c_new, k_new, cache_ckv, cache_kpe, page_table, wuv_all, head_mask,
                layer, n_new):
    n_seq, n_pages = page_table.shape
    n_steps = n_pages // PAGES_PER_STEP
    n_rows = q_lat.shape[1]

    def page_spec(width, k):
        return pl.BlockSpec((None, None, PAGE_SIZE, width),
                            lambda b, s, pt: (layer, pt[b, s * PAGES_PER_STEP + k], 0, 0))

    in_specs = [pl.BlockSpec((None, n_rows, KV_LORA), lambda b, s, pt: (b, 0, 0)),
                pl.BlockSpec((None, n_rows, QK_ROPE), lambda b, s, pt: (b, 0, 0)),
                pl.BlockSpec((None, SUBLANES, KV_LORA), lambda b, s, pt: (b, 0, 0)),
                pl.BlockSpec((None, SUBLANES, QK_ROPE), lambda b, s, pt: (b, 0, 0))]
    in_specs += [page_spec(KV_LORA, k) for k in range(PAGES_PER_STEP)]
    in_specs += [page_spec(QK_ROPE, k) for k in range(PAGES_PER_STEP)]
    in_specs += [pl.BlockSpec(wuv_all.shape, lambda b, s, pt: (0, 0)),
                 pl.BlockSpec(head_mask.shape, lambda b, s, pt: (0, 0))]
    grid_spec = pltpu.PrefetchScalarGridSpec(
        num_scalar_prefetch=1,
        grid=(n_seq, n_steps),
        in_specs=in_specs,
        out_specs=pl.BlockSpec((None, n_new, B_HEADS * V_HD), lambda b, s, pt: (b, 0, 0)),
        scratch_shapes=[pltpu.VMEM((n_rows, 1), F32), pltpu.VMEM((n_rows, 1), F32),
                        pltpu.VMEM((n_rows, KV_LORA), F32)],
    )
    return pl.pallas_call(
        functools.partial(_mla_s_kernel, n_new),
        grid_spec=grid_spec,
        out_shape=jax.ShapeDtypeStruct((n_seq, n_new, B_HEADS * V_HD), F32),
        compiler_params=_params("parallel", "arbitrary"),
        name="mla_sample",
    )(page_table, q_lat, q_pe, c_new, k_new, *([cache_ckv] * PAGES_PER_STEP),
      *([cache_kpe] * PAGES_PER_STEP), wuv_all, head_mask)


def _kc_kernel(x_ref, w_ref, gb_ref, q_out, k_out, v_out, o_out, g_out):
    proj = _dot(x_ref[...].astype(BF16), w_ref[...])
    q_out[...] = (proj[:, :OFF_MK] * (C_DK ** -0.5)).astype(BF16)
    k_out[...] = proj[:, OFF_MK:OFF_MV].astype(BF16)
    v_out[...] = proj[:, OFF_MV:OFF_MO].astype(BF16)
    o_out[...] = jax.nn.sigmoid(proj[:, OFF_MO:OFF_MG])
    pre = proj[:, OFF_MG:] + gb_ref[...]
    cap = GATE_CAP * jnp.tanh(pre / GATE_CAP)
    lane = lax.broadcasted_iota(jnp.int32, cap.shape, 1)
    g_out[...] = jnp.where(lane < C_HEADS, cap, jax.nn.log_sigmoid(cap))


def _layer1_prep(x, w_in, gate_bias, tm):
    rows = x.shape[0]
    widths = [OFF_MK, OFF_MK, MIX_O, MIX_O, LANES]
    dtypes = [BF16, BF16, BF16, F32, F32]
    return pl.pallas_call(
        _kc_kernel,
        grid=(rows // tm,),
        in_specs=[pl.BlockSpec((tm, D_MODEL), lambda i: (i, 0)), _full(w_in.shape),
                  _full(gate_bias.shape)],
        out_specs=[pl.BlockSpec((tm, wd), lambda i: (i, 0)) for wd in widths],
        out_shape=[jax.ShapeDtypeStruct((rows, wd), dt) for wd, dt in zip(widths, dtypes)],
        compiler_params=_params("parallel"),
        name="layer1_prep",
    )(x, w_in, gate_bias)


def _mlstm_kernel(chunk, q_ref, k_ref, v_ref, g_ref, c0_ref, n0_ref, m0_ref, h_ref, co_ref,
                  no_ref, mo_ref, c_s, n_s, m_s):
    ci = pl.program_id(1)

    @pl.when(ci == 0)
    def _():
        c_s[...] = c0_ref[...]
        n_s[...] = n0_ref[...]
        m_s[...] = m0_ref[...]

    g = g_ref[...]
    row = lax.broadcasted_iota(jnp.int32, g.shape, 0)
    lane = lax.broadcasted_iota(jnp.int32, g.shape, 1)
    b = g
    shift = 1
    while shift < chunk:
        b = b + jnp.where(row >= shift, pltpu.roll(b, shift, 0), 0.0)
        shift *= 2
    g_t = g.T
    b_t = b.T
    rr = lax.broadcasted_iota(jnp.int32, (chunk, chunk), 0)
    cc = lax.broadcasted_iota(jnp.int32, (chunk, chunk), 1)
    causal = cc <= rr
    for h in range(C_HEADS):
        b_col = jnp.sum(jnp.where(lane == C_HEADS + h, b, 0.0), axis=-1, keepdims=True)
        i_col = jnp.sum(jnp.where(lane == h, g, 0.0), axis=-1, keepdims=True)
        b_row = b_t[C_HEADS + h:C_HEADS + h + 1, :]
        i_row = g_t[h:h + 1, :]
        m_prev = m_s[h:h + 1, 0:1]
        d = jnp.where(causal, b_col - b_row + i_row, NEG_INF)
        gg = b_col + m_prev
        mt = jnp.maximum(gg, jnp.max(d, axis=-1, keepdims=True))
        w_inter = jnp.exp(gg - mt)
        qh = q_ref[:, h * HEAD_PAD:(h + 1) * HEAD_PAD]
        kh = k_ref[:, h * HEAD_PAD:(h + 1) * HEAD_PAD]
        vh = v_ref[:, h * C_DV:(h + 1) * C_DV]
        c_prev = c_s[h]
        n_prev = n_s[h:h + 1, :]
        att = jnp.exp(d - mt) * _dot_nt(qh, kh)
        num = w_inter * _dot_nt(qh, c_prev.astype(BF16)) + _dot(att.astype(BF16), vh)
        qn = jnp.sum(qh.astype(F32) * n_prev, axis=-1, keepdims=True)
        den = w_inter * qn + jnp.sum(att, axis=-1, keepdims=True)
        h_ref[:, h * C_DV:(h + 1) * C_DV] = num / jnp.maximum(jnp.abs(den), jnp.exp(-mt))
        m_new = mt[chunk - 1:chunk, :]
        b_last = b_col[chunk - 1:chunk, :]
        w_state = jnp.exp(b_last - b_col + i_col - m_new)
        dec = jnp.exp(b_last + m_prev - m_new)
        wv = (w_state * vh.astype(F32)).astype(BF16)
        c_s[h] = dec * c_prev + _dot_tn(wv, kh)
        n_s[h:h + 1, :] = dec * n_prev + jnp.sum(w_state * kh.astype(F32), axis=0, keepdims=True)
        m_s[h:h + 1, :] = jnp.broadcast_to(m_new, (1, LANES))

    @pl.when(ci == pl.num_programs(1) - 1)
    def _():
        co_ref[...] = c_s[...]
        no_ref[...] = n_s[...]
        mo_ref[...] = m_s[...]


def _mlstm(q, k, v, gates, c0, n0, m0, bsz, t_len, chunk):
    nc = t_len // chunk
    seq = lambda wd: pl.BlockSpec((chunk, wd), lambda b, c: (b * nc + c, 0))
    c_spec = pl.BlockSpec((None, C_HEADS, C_DV, HEAD_PAD), lambda b, c: (b, 0, 0, 0))
    v_spec = pl.BlockSpec((None, C_HEADS, LANES), lambda b, c: (b, 0, 0))
    return pl.pallas_call(
        functools.partial(_mlstm_kernel, chunk),
        grid=(bsz, nc),
        in_specs=[seq(OFF_MK), seq(OFF_MK), seq(MIX_O), seq(LANES), c_spec, v_spec, v_spec],
        out_specs=[seq(MIX_O), c_spec, v_spec, v_spec],
        out_shape=[jax.ShapeDtypeStruct((bsz * t_len, MIX_O), F32),
                   jax.ShapeDtypeStruct(c0.shape, F32),
                   jax.ShapeDtypeStruct(n0.shape, F32),
                   jax.ShapeDtypeStruct(m0.shape, F32)],
        scratch_shapes=[pltpu.VMEM((C_HEADS, C_DV, HEAD_PAD), F32),
                        pltpu.VMEM((C_HEADS, LANES), F32), pltpu.VMEM((C_HEADS, LANES), F32)],
        compiler_params=_params("parallel", "arbitrary"),
        name="mlstm",
    )(q, k, v, gates, c0, n0, m0)


def _kd_kernel(h_ref, o_ref, x_ref, mh_ref, wo_ref, ng_ref, nb_ref, out_ref):
    parts = []
    for h in range(C_HEADS):
        hh = h_ref[:, h * C_DV:(h + 1) * C_DV]
        parts.append(hh * lax.rsqrt(jnp.mean(hh * hh, axis=-1, keepdims=True) + 1e-6))
    hn = jnp.concatenate(parts, axis=-1) * mh_ref[...]
    y = _dot((hn * o_ref[...]).astype(BF16), wo_ref[...])
    out_ref[...] = _layer_norm_res(x_ref[...], y, ng_ref[...], nb_ref[...])


def _layer1_out(h, o_sig, x, mh_norm, w_out, ln_g, ln_b, tm):
    rows = x.shape[0]
    full = pl.BlockSpec((tm, D_MODEL), lambda i: (i, 0))
    consts = [mh_norm, w_out, ln_g, ln_b]
    return pl.pallas_call(
        _kd_kernel,
        grid=(rows // tm,),
        in_specs=[full, full, full] + [_full(c.shape) for c in consts],
        out_specs=full,
        out_shape=jax.ShapeDtypeStruct((rows, D_MODEL), F32),
        compiler_params=_params("parallel"),
        name="layer1_out",
    )(h, o_sig, x, *consts)


def _top16(s_ref, val_ref, idx_ref):
    n = s_ref.shape[0]
    row = lax.broadcasted_iota(jnp.int32, s_ref.shape, 0)
    s = s_ref[...]
    for r in range(PEER_TOPK):
        m = jnp.max(s, axis=0, keepdims=True)
        am = jnp.min(jnp.where(s == m, row, n), axis=0, keepdims=True)
        val_ref[r:r + 1, :] = m
        idx_ref[r:r + 1, :] = am
        s = jnp.where(row == am, NEG_INF, s)


def _pick_rank(src_ref, rank):
    out = jnp.zeros(rank.shape, jnp.int32)
    for r in range(PEER_TOPK):
        out = jnp.where(rank == r, src_ref[r:r + 1, :], out)
    return out


def _peer_q_kernel(tb, x_ref, wq_ref, k1_ref, k2_ref, i1_out, i2_out, g_out,
                   s_ref, cand_ref, t1_ref, j1_ref, t2_ref, j2_ref, tc_ref, jc_ref,
                   e1_ref, e2_ref, gt_ref):
    q = _dot(x_ref[...].astype(BF16), wq_ref[...])
    for h in range(PEER_HEADS):
        q1 = q[:, h * PEER_QDIM:h * PEER_QDIM + PEER_HALF].astype(BF16)
        q2 = q[:, h * PEER_QDIM + PEER_HALF:(h + 1) * PEER_QDIM].astype(BF16)
        s_ref[...] = _dot_nt(k1_ref[...], q1)
        _top16(s_ref, t1_ref, j1_ref)
        s_ref[...] = _dot_nt(k2_ref[...], q2)
        _top16(s_ref, t2_ref, j2_ref)
        for i in range(PEER_TOPK):
            cand_ref[i * PEER_TOPK:(i + 1) * PEER_TOPK, :] = t1_ref[i:i + 1, :] + t2_ref[...]
        _top16(cand_ref, tc_ref, jc_ref)
        flat = jc_ref[...]
        rows = slice(h * PEER_TOPK, (h + 1) * PEER_TOPK)
        e1_ref[rows, :] = _pick_rank(j1_ref, flat >> 4).astype(F32)
        e2_ref[rows, :] = _pick_rank(j2_ref, flat & (PEER_TOPK - 1)).astype(F32)
        top = tc_ref[...]
        e = jnp.exp(top - top[0:1, :])
        gt_ref[rows, :] = e / jnp.sum(e, axis=0, keepdims=True)
    i1_out[...] = e1_ref[...].T.astype(jnp.int32)
    i2_out[...] = e2_ref[...].T.astype(jnp.int32)
    g_out[...] = gt_ref[...].T


def _peer_select(x, wq, k1, k2, tb):
    rows = x.shape[0]
    out = pl.BlockSpec((tb, PEER_PAIRS), lambda i: (i, 0))
    small = lambda dt: pltpu.VMEM((PEER_TOPK, tb), dt)
    return pl.pallas_call(
        functools.partial(_peer_q_kernel, tb),
        grid=(rows // tb,),
        in_specs=[pl.BlockSpec((tb, D_MODEL), lambda i: (i, 0)), _full(wq.shape), _full(k1.shape),
                  _full(k2.shape)],
        out_specs=[out, out, out],
        out_shape=[jax.ShapeDtypeStruct((rows, PEER_PAIRS), jnp.int32),
                   jax.ShapeDtypeStruct((rows, PEER_PAIRS), jnp.int32),
                   jax.ShapeDtypeStruct((rows, PEER_PAIRS), F32)],
        scratch_shapes=[pltpu.VMEM((N_KEYS, tb), F32),
                        pltpu.VMEM((PEER_TOPK * PEER_TOPK, tb), F32),
                        small(F32), small(jnp.int32), small(F32), small(jnp.int32),
                        small(F32), small(jnp.int32),
                        pltpu.VMEM((PEER_PAIRS, tb), F32), pltpu.VMEM((PEER_PAIRS, tb), F32),
                        pltpu.VMEM((PEER_PAIRS, tb), F32)],
        compiler_params=_params("parallel"),
        name="peer_select",
    )(x, wq, k1, k2)


PEER_SPLIT = 2


def _peer_u_kernel(tb, n_i1, x_ref, u_ref, i1_ref, i2_ref, g_ref, a_ref, hs_ref):
    part = pl.program_id(0)
    xb = x_ref[...].astype(BF16)
    for pp in range(n_i1 // 2):
        h2 = _dot_nt(xb, u_ref[pp * 2 * N_KEYS:(pp + 1) * 2 * N_KEYS, :])
        hs_ref[(2 * pp) * tb:(2 * pp + 1) * tb, :] = h2[:, :N_KEYS]
        hs_ref[(2 * pp + 1) * tb:(2 * pp + 2) * tb, :] = h2[:, N_KEYS:]
    sub1 = lax.broadcasted_iota(jnp.int32, (n_i1, PEER_PAIRS), 0) + part * n_i1
    sub2 = lax.broadcasted_iota(jnp.int32, (N_KEYS, PEER_PAIRS), 0)

    def token(t, carry):
        i1 = i1_ref[pl.ds(t, 1), :]
        i2 = i2_ref[pl.ds(t, 1), :]
        gate = g_ref[pl.ds(t, 1), :]
        sel1 = jnp.where(sub1 == i1, gate, 0.0).astype(BF16)
        sel2 = jnp.where(sub2 == i2, 1.0, 0.0).astype(BF16)
        gmat = _dot_nt(sel1, sel2)
        rows = pl.ds(t, n_i1, stride=tb)
        ht = hs_ref[rows, :]
        act = 0.5 * ht * (1.0 + lax.erf(ht * (2.0 ** -0.5)))
        hs_ref[rows, :] = act * gmat
        return carry

    lax.fori_loop(0, tb, token, 0)
    for i in range(n_i1):
        a_ref[:, i * N_KEYS:(i + 1) * N_KEYS] = hs_ref[i * tb:(i + 1) * tb, :].astype(BF16)


def _peer_act(x, u, i1, i2, gate, tb):
    rows = x.shape[0]
    n_i1 = N_KEYS // PEER_SPLIT
    n_exp = N_EXPERTS // PEER_SPLIT
    tok = lambda wd: pl.BlockSpec((tb, wd), lambda p, i: (i, 0))
    return pl.pallas_call(
        functools.partial(_peer_u_kernel, tb, n_i1),
        grid=(PEER_SPLIT, rows // tb),
        in_specs=[tok(D_MODEL), pl.BlockSpec((n_exp, D_MODEL), lambda p, i: (p, 0)),
                  tok(PEER_PAIRS), tok(PEER_PAIRS), tok(PEER_PAIRS)],
        out_specs=pl.BlockSpec((tb, n_exp), lambda p, i: (i, p)),
        out_shape=jax.ShapeDtypeStruct((rows, N_EXPERTS), BF16),
        scratch_shapes=[pltpu.VMEM((n_i1 * tb, N_KEYS), F32)],
        compiler_params=_params("arbitrary", "arbitrary"),
        name="peer_act",
    )(x, u, i1, i2, gate)


def _peer_v_kernel(a_ref, v_ref, x_ref, ng_ref, nb_ref, o_ref, acc_ref):
    kk = pl.program_id(1)

    @pl.when(kk == 0)
    def _():
        acc_ref[...] = jnp.zeros_like(acc_ref)

    acc_ref[...] += _dot(a_ref[...], v_ref[...])

    @pl.when(kk == pl.num_programs(1) - 1)
    def _():
        o_ref[...] = _layer_norm_res(x_ref[...], acc_ref[...], ng_ref[...], nb_ref[...])


def _peer_out(a, v, x, ln_g, ln_b, tm, tk):
    rows = x.shape[0]
    return pl.pallas_call(
        _peer_v_kernel,
        grid=(rows // tm, N_EXPERTS // tk),
        in_specs=[pl.BlockSpec((tm, tk), lambda i, k: (i, k)),
                  pl.BlockSpec((tk, D_MODEL), lambda i, k: (k, 0)),
                  pl.BlockSpec((tm, D_MODEL), lambda i, k: (i, 0)),
                  pl.BlockSpec(ln_g.shape, lambda i, k: (0, 0)),
                  pl.BlockSpec(ln_b.shape, lambda i, k: (0, 0))],
        out_specs=pl.BlockSpec((tm, D_MODEL), lambda i, k: (i, 0)),
        out_shape=jax.ShapeDtypeStruct((rows, D_MODEL), F32),
        scratch_shapes=[pltpu.VMEM((tm, D_MODEL), F32)],
        compiler_params=_params("parallel", "arbitrary"),
        name="peer_out",
    )(a, v, x, ln_g, ln_b)


def _peer(x, pw, ln_g, ln_b, tb_sel, tb_act, tm_out):
    rows = x.shape[0]
    i1, i2, gate = _peer_select(x, pw["wq"], pw["k1"], pw["k2"], min(tb_sel, rows))
    act = _peer_act(x, pw["u"], i1, i2, gate, min(tb_act, rows))
    return _peer_out(act, pw["v"], x, ln_g, ln_b, min(tm_out, rows), 2048)


def _pad_cols(w, width):
    return jnp.pad(w, ((0, 0), (0, width - w.shape[1])))


def _row(v):
    return v.reshape(1, -1).astype(F32)


def _layer0_weights(w_in, mu, w0, w2, a0, a2, g2, kk, ka, rk, lnx_w, lnx_b, q_norm, w_uq, kv_norm,
                    w_uk, w_uv, w_out):
    c0 = 3 * A_W
    kpe0 = RWKV_COLS + Q_LORA + KV_LORA
    tile = lambda cols: jnp.tile(cols, (1, B_HEADS))
    w_in_p = jnp.concatenate([
        w_in[:, :c0],
        _pad_cols(w_in[:, c0:c0 + LORA_W], LANES),
        _pad_cols(w_in[:, c0 + LORA_W:c0 + LORA_W + LORA_A], LANES),
        w_in[:, c0 + LORA_W + LORA_A:RWKV_COLS],
        w_in[:, RWKV_COLS:kpe0],
        tile(w_in[:, kpe0:kpe0 + ROPE_HALF]),
        tile(w_in[:, kpe0 + ROPE_HALF:kpe0 + QK_ROPE]),
    ], axis=1).astype(BF16)
    mu_p = jnp.concatenate([mu[:c0], jnp.pad(mu[c0:c0 + LORA_W], (0, LANES - LORA_W)),
                            jnp.pad(mu[c0 + LORA_W:c0 + LORA_W + LORA_A], (0, LANES - LORA_A)),
                            mu[c0 + LORA_W + LORA_A:]])
    wq3 = w_uq.reshape(Q_LORA, B_HEADS, QK_NOPE + QK_ROPE)
    nope = jnp.pad(wq3[:, :, :QK_NOPE], ((0, 0), (0, 0), (0, LANES - QK_NOPE)))
    w_uq_p = jnp.concatenate([
        nope.reshape(Q_LORA, B_HEADS * LANES),
        wq3[:, :, QK_NOPE:QK_NOPE + ROPE_HALF].reshape(Q_LORA, LANES),
        wq3[:, :, QK_NOPE + ROPE_HALF:].reshape(Q_LORA, LANES),
    ], axis=1).astype(BF16)
    w_uk_t = jnp.pad(jnp.transpose(w_uk, (1, 2, 0)), ((0, 0), (0, LANES - QK_NOPE), (0, 0)))
    heads = np.arange(B_HEADS)
    blk = np.zeros((B_HEADS, KV_LORA, B_HEADS * V_HD), np.float32)
    hm = np.zeros((B_HEADS, B_HEADS * V_HD), np.float32)
    pem = np.zeros((B_HEADS, 2 * LANES), np.float32)
    for h in heads:
        blk[h, :, h * V_HD:(h + 1) * V_HD] = 1.0
        hm[h, h * V_HD:(h + 1) * V_HD] = 1.0
        pem[h, h * ROPE_HALF:(h + 1) * ROPE_HALF] = 1.0
        pem[h, LANES + h * ROPE_HALF:LANES + (h + 1) * ROPE_HALF] = 1.0
    wuv_all = w_uv.reshape(KV_LORA, B_HEADS * V_HD)
    seg = np.kron(np.eye(A_HEADS, dtype=np.float32), np.ones((A_HD, A_HD), np.float32))
    return {
        "w_in": w_in_p, "mu": _row(mu_p), "w0": _row(w0),
        "w2": jnp.pad(w2, ((0, LANES - LORA_W), (0, 0))).astype(BF16), "a0": _row(a0),
        "a2": jnp.pad(a2, ((0, LANES - LORA_A), (0, 0))).astype(BF16), "g2": g2.astype(BF16),
        "kk": _row(kk), "ka": _row(ka), "rk": _row(rk), "lnx_w": _row(lnx_w), "lnx_b": _row(lnx_b),
        "q_norm": _row(q_norm), "w_uq": w_uq_p, "kv_norm": _row(kv_norm),
        "w_uk": w_uk_t.astype(BF16),
        "wuv_pad": (wuv_all[None] * jnp.asarray(blk)).astype(BF16),
        "wuv_all": wuv_all.astype(BF16), "head_mask": jnp.asarray(hm),
        "pe_mask": jnp.asarray(pem), "seg": jnp.asarray(seg, dtype=BF16),
        "w_out": w_out.astype(BF16),
    }


def _layer1_weights(w_in, b_i, b_f, mh_norm, w_out):
    hk = C_HEADS * C_DK
    padh = lambda w: jnp.pad(w.reshape(D_MODEL, C_HEADS, C_DK),
                             ((0, 0), (0, 0), (0, HEAD_PAD - C_DK))).reshape(D_MODEL, OFF_MK)
    w_in_p = jnp.concatenate([
        padh(w_in[:, :hk]), padh(w_in[:, hk:2 * hk]), w_in[:, 2 * hk:2 * hk + 2 * MIX_O],
        _pad_cols(w_in[:, 2 * hk + 2 * MIX_O:], LANES),
    ], axis=1).astype(BF16)
    gate_bias = jnp.pad(jnp.concatenate([b_i, b_f]), (0, LANES - 2 * C_HEADS)).reshape(1, LANES)
    return {"w_in": w_in_p, "gate_bias": gate_bias.astype(F32), "mh_norm": _row(mh_norm),
            "w_out": w_out.astype(BF16)}


def _rope_tables(pos):
    inv = ROPE_BASE ** (-jnp.arange(0, QK_ROPE, 2, dtype=F32) / QK_ROPE)
    ang = pos.astype(F32)[:, None] * inv[None, :]
    return jnp.tile(jnp.cos(ang), (1, B_HEADS)), jnp.tile(jnp.sin(ang), (1, B_HEADS))


def _unpack_kpe(kpe_rot, bsz, t_len):
    return jnp.concatenate([kpe_rot[:, :ROPE_HALF], kpe_rot[:, LANES:LANES + ROPE_HALF]],
                           axis=-1).reshape(bsz, t_len, QK_ROPE)


def kernel(x_prompt, x_sample, cache_ckv, cache_kpe, page_table, state_shift, state_wkv, state_mlstm_c, state_mlstm_n, state_mlstm_m, w_in_e, mu_e, w0_e, w2_e, a0_e, a2_e, g2_e, kk_e, ka_e, rk_e, lnx_w_e, lnx_b_e, q_norm_e, w_uq_e, kv_norm_e, w_uk_e, w_uv_e, w_out_e, w_in_o, b_i_o, b_f_o, mh_norm_o, w_out_o, peer_wq, peer_k1, peer_k2, peer_u, peer_v, ln1_g, ln1_b, ln2_g, ln2_b):
    bp, tp, _ = x_prompt.shape
    bs, ts, _ = x_sample.shape
    past_len = page_table.shape[1] * PAGE_SIZE
    xp = x_prompt.reshape(bp * tp, D_MODEL)
    xs = x_sample.reshape(bs * ts, D_MODEL)
    rows_s = bs * ts
    tm_s = min(256, rows_s)
    tm_p = 256

    e = 0
    w0 = _layer0_weights(w_in_e[e], mu_e[e], w0_e[e], w2_e[e], a0_e[e], a2_e[e], g2_e[e], kk_e[e],
                         ka_e[e], rk_e[e], lnx_w_e[e], lnx_b_e[e], q_norm_e[e], w_uq_e[e],
                         kv_norm_e[e], w_uk_e[e], w_uv_e[e], w_out_e[e])
    g1, b1 = _row(ln1_g[0]), _row(ln1_b[0])
    cos_p, sin_p = _rope_tables(jnp.arange(tp))
    cos_s, sin_s = _rope_tables(past_len + jnp.arange(ts))
    cos_s = jnp.tile(cos_s, (tm_s // ts, 1))
    sin_s = jnp.tile(sin_s, (tm_s // ts, 1))

    (r, w, k, v, kk, kka, g, bonus, qcat, _, c_p, kpe_p, kcat) = _layer0_prep(
        xp, None, w0, cos_p, sin_p, bp, tp // tm_p, tm_p, tp)
    ch = lambda a: _to_chains(a, bp, tp)
    s0_p = jnp.zeros((A_HD, A_HD, bp * A_HEADS), F32)
    y, s_p = _wkv_scan(ch(r), ch(w), ch(k), ch(v), ch(kk), ch(kka), s0_p, 32)
    y = _from_chains(y, bp, tp)
    out_b = _mla_prompt(qcat, kcat, w0["wuv_pad"], bp, tp, 256)
    xp1 = _layer0_out(y, g, bonus, out_b, xp, w0, g1, b1, tm_p)
    wkv_p = jnp.transpose(s_p.reshape(A_HD, A_HD, bp, A_HEADS), (2, 3, 1, 0))

    start = _mm(state_shift[e], w0["w_in"][:, :RW_PAD], bs)
    start = jnp.repeat(start, ts, axis=0)
    (r, w, k, v, kk, kka, g, bonus, qcat, qpe, c_s, kpe_s, _) = _layer0_prep(
        xs, start, w0, cos_s, sin_s, rows_s // tm_s, 1, tm_s, ts)
    ch = lambda a: _to_chains(a, bs, ts)
    s0_s = jnp.transpose(state_wkv[e], (3, 2, 0, 1)).reshape(A_HD, A_HD, bs * A_HEADS)
    y, s_s = _wkv_scan(ch(r), ch(w), ch(k), ch(v), ch(kk), ch(kka), s0_s, ts)
    y = _from_chains(y, bs, ts)
    q_lat = qcat.reshape(bs, ts * B_HEADS, QCAT)[:, :, :KV_LORA]
    q_pe = jnp.transpose(qpe.reshape(bs, ts, 2, B_HEADS, ROPE_HALF), (0, 1, 3, 2, 4))
    q_pe = q_pe.reshape(bs, ts * B_HEADS, QK_ROPE).astype(BF16)
    kpe_new = _unpack_kpe(kpe_s, bs, ts)
    pad_new = lambda a: jnp.pad(a, ((0, 0), (0, SUBLANES - ts), (0, 0)))
    out_b = _mla_sample(q_lat, q_pe, pad_new(c_s.reshape(bs, ts, KV_LORA)), pad_new(kpe_new),
                        cache_ckv, cache_kpe, page_table, w0["wuv_all"], w0["head_mask"], e, ts)
    xs1 = _layer0_out(y, g, bonus, out_b.reshape(rows_s, B_HEADS * V_HD), xs, w0, g1, b1, tm_s)
    wkv_s = jnp.transpose(s_s.reshape(A_HD, A_HD, bs, A_HEADS), (2, 3, 1, 0))

    new_ckv_prompt = c_p.reshape(1, bp, tp, KV_LORA)
    new_kpe_prompt = _unpack_kpe(kpe_p, bp, tp)[None]
    new_ckv_sample = c_s.reshape(1, bs, ts, KV_LORA)
    new_kpe_sample = kpe_new[None]
    new_shift_prompt = x_prompt[:, -1][None]
    new_shift_sample = x_sample[:, -1][None]

    def peer_weights(l):
        return {"wq": peer_wq[l].astype(BF16), "k1": peer_k1[l].astype(BF16),
                "k2": peer_k2[l].astype(BF16), "u": peer_u[l].astype(BF16),
                "v": peer_v[l].astype(BF16)}

    pw = peer_weights(0)
    g2n, b2n = _row(ln2_g[0]), _row(ln2_b[0])
    xp2 = _peer(xp1, pw, g2n, b2n, 256, 128, 512)
    xs2 = _peer(xs1, pw, g2n, b2n, 256, 128, 512)

    od = 0
    w1 = _layer1_weights(w_in_o[od], b_i_o[od], b_f_o[od], mh_norm_o[od], w_out_o[od])
    g1, b1 = _row(ln1_g[1]), _row(ln1_b[1])
    pad_dk = lambda a: jnp.pad(a, [(0, 0)] * (a.ndim - 1) + [(0, HEAD_PAD - C_DK)])

    q, k, v, o_sig, gates = _layer1_prep(xp2, w1["w_in"], w1["gate_bias"], tm_p)
    zc = jnp.zeros((bp, C_HEADS, C_DV, HEAD_PAD), F32)
    zn = jnp.zeros((bp, C_HEADS, LANES), F32)
    h, c_p1, n_p1, m_p1 = _mlstm(q, k, v, gates, zc, zn, zn, bp, tp, MLSTM_CHUNK)
    xp3 = _layer1_out(h, o_sig, xp2, w1["mh_norm"], w1["w_out"], g1, b1, tm_p)

    q, k, v, o_sig, gates = _layer1_prep(xs2, w1["w_in"], w1["gate_bias"], tm_s)
    lp = MLSTM_CHUNK
    pad_t = lambda a: jnp.pad(a.reshape(bs, ts, -1), ((0, 0), (0, lp - ts), (0, 0))).reshape(
        bs * lp, -1)
    lane = jnp.arange(LANES)
    gate_fill = jnp.where(lane < C_HEADS, GATE_PAD, 0.0).astype(F32)
    gates_p = jnp.concatenate(
        [gates.reshape(bs, ts, LANES), jnp.broadcast_to(gate_fill, (bs, lp - ts, LANES))],
        axis=1).reshape(bs * lp, LANES)
    m0 = jnp.broadcast_to(state_mlstm_m[od][:, :, None], (bs, C_HEADS, LANES))
    h, c_s1, n_s1, m_s1 = _mlstm(pad_t(q), pad_t(k), pad_t(v), gates_p, pad_dk(state_mlstm_c[od]),
                                 pad_dk(state_mlstm_n[od]), m0, bs, lp, lp)
    h = h.reshape(bs, lp, MIX_O)[:, :ts].reshape(rows_s, MIX_O)
    xs3 = _layer1_out(h, o_sig, xs2, w1["mh_norm"], w1["w_out"], g1, b1, tm_s)

    pw = peer_weights(1)
    g2n, b2n = _row(ln2_g[1]), _row(ln2_b[1])
    xp4 = _peer(xp3, pw, g2n, b2n, 256, 128, 512)
    xs4 = _peer(xs3, pw, g2n, b2n, 256, 128, 512)

    return (xp4.reshape(bp, tp, D_MODEL), xs4.reshape(bs, ts, D_MODEL),
            new_ckv_prompt, new_kpe_prompt, new_ckv_sample, new_kpe_sample,
            new_shift_prompt, new_shift_sample, wkv_p[None], wkv_s[None],
            c_p1[..., :C_DK][None], c_s1[..., :C_DK][None],
            n_p1[..., :C_DK][None], n_s1[..., :C_DK][None],
            m_p1[..., 0][None], m_s1[..., 0][None])
```

```python
import functools

import jax
import jax.numpy as jnp
import numpy as np
from jax import lax
from jax.experimental import pallas as pl
from jax.experimental.pallas import tpu as pltpu

F32 = jnp.float32
BF16 = jnp.bfloat16
NEG_INF = float("-inf")

LANES = 128
SUBLANES = 8
VMEM_LIMIT_BYTES = 56 * 1024 * 1024

D_MODEL = 1024
DEPTH = 2
PAGE_SIZE = 128
ALPHA = (2 * DEPTH) ** 0.25
LN_EPS = 1e-5
A_HEADS, A_HD = 8, 64
A_W = A_HEADS * A_HD
LORA_W, LORA_A, LORA_G = 64, 64, 128
RWKV_COLS = 3 * A_W + LORA_W + LORA_A + LORA_G
GN_EPS = 64e-5
B_HEADS, QK_NOPE, QK_ROPE, V_HD = 8, 64, 32, 64
Q_LORA, KV_LORA = 384, 256
SM_SCALE = (QK_NOPE + QK_ROPE) ** -0.5
ROPE_BASE = 10000.0
ROPE_HALF = QK_ROPE // 2
C_HEADS, C_DK, C_DV = 8, 64, 128
MIX_O = C_HEADS * C_DV
MLSTM_CHUNK = 64
GATE_CAP = 15.0
GATE_PAD = -1e30
N_KEYS = 128
N_EXPERTS = N_KEYS * N_KEYS
PEER_HEADS, PEER_QDIM, PEER_TOPK = 8, 256, 16
PEER_HALF = PEER_QDIM // 2
PEER_PAIRS = PEER_HEADS * PEER_TOPK

RW_PAD = 3 * A_W + 3 * LANES
OFF_CQ = RW_PAD
OFF_CKV = OFF_CQ + Q_LORA
OFF_KP1 = OFF_CKV + KV_LORA
OFF_KP2 = OFF_KP1 + LANES
IN_E_PAD = OFF_KP2 + LANES
Q_PAD = B_HEADS * LANES + 2 * LANES
QCAT = 2 * KV_LORA
HEAD_PAD = LANES
OFF_MK = C_HEADS * HEAD_PAD
OFF_MV = 2 * C_HEADS * HEAD_PAD
OFF_MO = OFF_MV + MIX_O
OFF_MG = OFF_MO + MIX_O
IN_O_PAD = OFF_MG + LANES


def _params(*sem):
    return pltpu.CompilerParams(dimension_semantics=sem, vmem_limit_bytes=VMEM_LIMIT_BYTES)


def _dot(a, b):
    return jnp.dot(a, b, preferred_element_type=F32)


def _dot_nt(a, b):
    return lax.dot_general(a, b, (((1,), (1,)), ((), ())), preferred_element_type=F32)


def _dot_tn(a, b):
    return lax.dot_general(a, b, (((0,), (0,)), ((), ())), preferred_element_type=F32)


def _seg_sum(x, seg):
    hi = x.astype(BF16)
    lo = (x - hi.astype(F32)).astype(BF16)
    return _dot(hi, seg) + _dot(lo, seg)


def _layer_norm_res(x, h, g, b):
    z = ALPHA * x + h
    mu = jnp.mean(z, axis=-1, keepdims=True)
    zc = z - mu
    var = jnp.mean(zc * zc, axis=-1, keepdims=True)
    return zc * lax.rsqrt(var + LN_EPS) * g + b


def _full(shape):
    nd = len(shape)
    return pl.BlockSpec(shape, lambda *_: (0,) * nd)


def _mm_kernel(x_ref, w_ref, o_ref):
    o_ref[...] = _dot(x_ref[...].astype(BF16), w_ref[...])


def _mm(x, w, tm):
    m, k = x.shape
    n = w.shape[1]
    return pl.pallas_call(
        _mm_kernel,
        grid=(m // tm,),
        in_specs=[pl.BlockSpec((tm, k), lambda i: (i, 0)), _full((k, n))],
        out_specs=pl.BlockSpec((tm, n), lambda i: (i, 0)),
        out_shape=jax.ShapeDtypeStruct((m, n), F32),
        compiler_params=_params("parallel"),
        name="mm",
    )(x, w)


def _ka_kernel(has_start, seq_len, tm, *refs):
    if has_start:
        x_ref, start_ref = refs[:2]
        refs = refs[2:]
    else:
        x_ref = refs[0]
        start_ref = None
        refs = refs[1:]
    (w_in_ref, mu_ref, w0_ref, w2_ref, a0_ref, a2_ref, g2_ref, kk_ref, ka_ref, rk_ref, qn_ref,
     wuq_ref, kvn_ref, wuk_ref, cos_ref, sin_ref, seg_ref, pem_ref,
     r_out, w_out, k_out, v_out, kk_out, kka_out, g_out, bonus_out, qcat_out, qpe_out, c_out,
     kpe_out, kcat_out, carry_ref) = refs
    t_blk = pl.program_id(1)
    proj = _dot(x_ref[...].astype(BF16), w_in_ref[...])
    p_rw = proj[:, :RW_PAD]
    row = lax.broadcasted_iota(jnp.int32, (tm, 1), 0)
    prev = pltpu.roll(p_rw, 1, 0)
    if has_start:
        prev = jnp.where(row % seq_len == 0, start_ref[...], prev)
    else:
        @pl.when(t_blk == 0)
        def _():
            carry_ref[...] = jnp.zeros_like(carry_ref)

        prev = jnp.where(row == 0, carry_ref[SUBLANES - 1:SUBLANES, :], prev)
        carry_ref[...] = p_rw[tm - SUBLANES:, :]
    p = p_rw + (prev - p_rw) * mu_ref[...]
    r = p[:, 0:A_W]
    k = p[:, A_W:2 * A_W]
    v = p[:, 2 * A_W:3 * A_W]
    lw = p[:, 3 * A_W:3 * A_W + LANES]
    la = p[:, 3 * A_W + LANES:3 * A_W + 2 * LANES]
    lg = p[:, 3 * A_W + 2 * LANES:RW_PAD]
    w_pre = w0_ref[...] + _dot(jnp.tanh(lw).astype(BF16), w2_ref[...])
    w_log = -jax.nn.softplus(-w_pre) - 0.5
    decay = jnp.exp(-jnp.exp(w_log))
    a = jax.nn.sigmoid(a0_ref[...] + _dot(la.astype(BF16), a2_ref[...]))
    g = _dot(jax.nn.sigmoid(lg).astype(BF16), g2_ref[...])
    seg = seg_ref[...]
    kk = k * kk_ref[...]
    kk = kk / jnp.maximum(jnp.sqrt(_seg_sum(kk * kk, seg)), 1e-12)
    k = k * (1.0 + (a - 1.0) * ka_ref[...])
    r_out[...] = r
    w_out[...] = decay
    k_out[...] = k
    v_out[...] = v
    kk_out[...] = kk
    kka_out[...] = kk * a
    g_out[...] = g
    bonus_out[...] = _seg_sum(r * k * rk_ref[...], seg) * v
    cq = proj[:, OFF_CQ:OFF_CKV]
    ckv = proj[:, OFF_CKV:OFF_KP1]
    kp1 = proj[:, OFF_KP1:OFF_KP2]
    kp2 = proj[:, OFF_KP2:IN_E_PAD]
    cos = cos_ref[...]
    sin = sin_ref[...]
    cqn = cq * lax.rsqrt(jnp.mean(cq * cq, axis=-1, keepdims=True) + 1e-6) * qn_ref[...]
    q = _dot(cqn.astype(BF16), wuq_ref[...])
    x1 = q[:, B_HEADS * LANES:B_HEADS * LANES + LANES]
    x2 = q[:, B_HEADS * LANES + LANES:Q_PAD]
    qpe = jnp.concatenate([x1 * cos - x2 * sin, x1 * sin + x2 * cos], axis=-1)
    qpe_out[...] = qpe
    for h in range(B_HEADS):
        q_lat = _dot(q[:, h * LANES:(h + 1) * LANES].astype(BF16), wuk_ref[h])
        qcat_out[:, h * QCAT:h * QCAT + KV_LORA] = q_lat.astype(BF16)
        qcat_out[:, h * QCAT + KV_LORA:(h + 1) * QCAT] = (qpe * pem_ref[h:h + 1, :]).astype(BF16)
    c = ckv * lax.rsqrt(jnp.mean(ckv * ckv, axis=-1, keepdims=True) + 1e-6) * kvn_ref[...]
    kpe = jnp.concatenate([kp1 * cos - kp2 * sin, kp1 * sin + kp2 * cos], axis=-1)
    c_out[...] = c
    kpe_out[...] = kpe
    kcat_out[:, :KV_LORA] = c.astype(BF16)
    kcat_out[:, KV_LORA:] = kpe.astype(BF16)


def _layer0_prep(x, start, wts, cos, sin, n_seq_blocks, n_t_blocks, tm, seq_len):
    rows = x.shape[0]
    has_start = start is not None
    rmap = lambda b, t: (b * n_t_blocks + t, 0)
    tmap = lambda b, t: (t, 0)
    ins = [x] + ([start] if has_start else [])
    in_specs = [pl.BlockSpec((tm, D_MODEL), rmap)]
    if has_start:
        in_specs.append(pl.BlockSpec((tm, RW_PAD), rmap))
    w_names = ("w_in", "mu", "w0", "w2", "a0", "a2", "g2", "kk", "ka", "rk", "q_norm", "w_uq",
               "kv_norm", "w_uk")
    for nme in w_names:
        ins.append(wts[nme])
        in_specs.append(_full(wts[nme].shape))
    ins += [cos, sin, wts["seg"], wts["pe_mask"]]
    in_specs += [pl.BlockSpec((tm, LANES), tmap), pl.BlockSpec((tm, LANES), tmap),
                 _full(wts["seg"].shape), _full(wts["pe_mask"].shape)]
    widths = [A_W] * 8 + [B_HEADS * QCAT, 2 * LANES, KV_LORA, 2 * LANES, QCAT]
    dtypes = [F32] * 8 + [BF16, F32, F32, F32, BF16]
    out_shape = [jax.ShapeDtypeStruct((rows, wd), dt) for wd, dt in zip(widths, dtypes)]
    out_specs = [pl.BlockSpec((tm, wd), rmap) for wd in widths]
    return pl.pallas_call(
        functools.partial(_ka_kernel, has_start, seq_len, tm),
        grid=(n_seq_blocks, n_t_blocks),
        in_specs=in_specs,
        out_specs=out_specs,
        out_shape=out_shape,
        scratch_shapes=[pltpu.VMEM((SUBLANES, RW_PAD), F32)],
        compiler_params=_params("parallel", "arbitrary"),
        name="layer0_prep",
    )(*ins)


def _wkv_kernel(tch, r_ref, w_ref, k_ref, v_ref, kk_ref, kka_ref, s0_ref, y_ref, sout_ref,
                s_ref, sa_ref):
    tc = pl.program_id(1)

    @pl.when(tc == 0)
    def _():
        s_ref[...] = s0_ref[...]

    acc = jnp.zeros((A_HD, LANES), F32)
    for j in range(A_HD):
        acc = acc + s_ref[j] * kk_ref[0, pl.ds(j, 1), :]
    sa_ref[...] = -acc

    def step(t, carry):
        tn = jnp.minimum(t + 1, tch - 1)
        sa = sa_ref[...]
        v_t = v_ref[t]
        y = jnp.zeros((A_HD, LANES), F32)
        san = jnp.zeros((A_HD, LANES), F32)
        for j in range(A_HD):
            row = pl.ds(j, 1)
            sj = s_ref[j] * w_ref[t, row, :] + sa * kka_ref[t, row, :] + v_t * k_ref[t, row, :]
            s_ref[j] = sj
            y = y + sj * r_ref[t, row, :]
            san = san + sj * kk_ref[tn, row, :]
        y_ref[t] = y
        sa_ref[...] = -san
        return carry

    lax.fori_loop(0, tch, step, 0)

    @pl.when(tc == pl.num_programs(1) - 1)
    def _():
        sout_ref[...] = s_ref[...]


def _wkv_scan(r, w, k, v, kk, kka, s0, tch):
    t_len, _, n_chains = r.shape
    pad = (-n_chains) % LANES
    if pad:
        padc = lambda a: jnp.pad(a, ((0, 0), (0, 0), (0, pad)))
        y, s = _wkv_scan(*(padc(a) for a in (r, w, k, v, kk, kka, s0)), tch)
        return y[..., :n_chains], s[..., :n_chains]
    chains = n_chains
    seq = pl.BlockSpec((tch, A_HD, LANES), lambda c, t: (t, 0, c))
    st = pl.BlockSpec((A_HD, A_HD, LANES), lambda c, t: (0, 0, c))
    return pl.pallas_call(
        functools.partial(_wkv_kernel, tch),
        grid=(chains // LANES, t_len // tch),
        in_specs=[seq] * 6 + [st],
        out_specs=[seq, st],
        out_shape=[jax.ShapeDtypeStruct((t_len, A_HD, chains), F32),
                   jax.ShapeDtypeStruct((A_HD, A_HD, chains), F32)],
        scratch_shapes=[pltpu.VMEM((A_HD, A_HD, LANES), F32), pltpu.VMEM((A_HD, LANES), F32)],
        compiler_params=_params("parallel", "arbitrary"),
        name="wkv_scan",
    )(r, w, k, v, kk, kka, s0)


def _to_chains(x, bsz, t_len):
    x = x.reshape(bsz, t_len, A_HEADS, A_HD)
    return jnp.transpose(x, (1, 3, 0, 2)).reshape(t_len, A_HD, bsz * A_HEADS)


def _from_chains(y, bsz, t_len):
    y = y.reshape(t_len, A_HD, bsz, A_HEADS)
    return jnp.transpose(y, (2, 0, 3, 1)).reshape(bsz * t_len, A_W)


def _kb_kernel(y_ref, g_ref, bonus_ref, ob_ref, x_ref, lw_ref, lb_ref, seg_ref, wo_ref, ng_ref,
               nb_ref, o_ref):
    seg = seg_ref[...]
    y = y_ref[...]
    mu = _seg_sum(y, seg) * (1.0 / A_HD)
    yc = y - mu
    var = _seg_sum(yc * yc, seg) * (1.0 / A_HD)
    y = yc * lax.rsqrt(var + GN_EPS) * lw_ref[...] + lb_ref[...] + bonus_ref[...]
    out_a = (y * g_ref[...]).astype(BF16)
    h = _dot(out_a, wo_ref[:A_W, :]) + _dot(ob_ref[...].astype(BF16), wo_ref[A_W:, :])
    o_ref[...] = _layer_norm_res(x_ref[...], h, ng_ref[...], nb_ref[...])


def _layer0_out(y, g, bonus, out_b, x, wts, ln_g, ln_b, tm):
    rows = x.shape[0]
    half = pl.BlockSpec((tm, A_W), lambda i: (i, 0))
    full = pl.BlockSpec((tm, D_MODEL), lambda i: (i, 0))
    consts = [wts["lnx_w"], wts["lnx_b"], wts["seg"], wts["w_out"], ln_g, ln_b]
    return pl.pallas_call(
        _kb_kernel,
        grid=(rows // tm,),
        in_specs=[half, half, half, half, full] + [_full(c.shape) for c in consts],
        out_specs=full,
        out_shape=jax.ShapeDtypeStruct((rows, D_MODEL), F32),
        compiler_params=_params("parallel"),
        name="layer0_out",
    )(y, g, bonus, out_b, x, *consts)


def _mla_p_kernel(tq, q_ref, kc_ref, wuv_ref, o_ref, m_ref, l_ref, acc_ref):
    qi = pl.program_id(1)
    rr = lax.broadcasted_iota(jnp.int32, (tq, tq), 0)
    cc = lax.broadcasted_iota(jnp.int32, (tq, tq), 1)
    causal = cc <= rr
    m_ref[...] = jnp.full(m_ref.shape, NEG_INF, F32)
    l_ref[...] = jnp.zeros(l_ref.shape, F32)
    acc_ref[...] = jnp.zeros(acc_ref.shape, F32)

    def chunk(c, mask):
        kc = kc_ref[pl.ds(pl.multiple_of(c * tq, tq), tq), :]
        for h in range(B_HEADS):
            s = _dot_nt(q_ref[:, h * QCAT:(h + 1) * QCAT], kc) * SM_SCALE
            if mask:
                s = jnp.where(causal, s, NEG_INF)
            m = m_ref[h]
            m_new = jnp.maximum(m, jnp.max(s, axis=-1, keepdims=True))
            alpha = jnp.exp(m - m_new)
            p = jnp.exp(s - m_new)
            m_ref[h] = m_new
            l_ref[h] = alpha * l_ref[h] + jnp.sum(p, axis=-1, keepdims=True)
            acc_ref[h] = alpha * acc_ref[h] + _dot(p.astype(BF16), kc[:, :KV_LORA])

    def body(c, carry):
        chunk(c, False)
        return carry

    lax.fori_loop(0, qi, body, 0)
    chunk(qi, True)
    out = jnp.zeros((tq, B_HEADS * V_HD), F32)
    for h in range(B_HEADS):
        out = out + _dot((acc_ref[h] / l_ref[h]).astype(BF16), wuv_ref[h])
    o_ref[...] = out


def _mla_prompt(qcat, kcat, wuv_pad, bsz, t_len, tq):
    nq = t_len // tq
    return pl.pallas_call(
        functools.partial(_mla_p_kernel, tq),
        grid=(bsz, nq),
        in_specs=[pl.BlockSpec((tq, B_HEADS * QCAT), lambda b, i: (b * nq + i, 0)),
                  pl.BlockSpec((t_len, QCAT), lambda b, i: (b, 0)),
                  _full(wuv_pad.shape)],
        out_specs=pl.BlockSpec((tq, B_HEADS * V_HD), lambda b, i: (b * nq + i, 0)),
        out_shape=jax.ShapeDtypeStruct((bsz * t_len, B_HEADS * V_HD), F32),
        scratch_shapes=[pltpu.VMEM((B_HEADS, tq, 1), F32), pltpu.VMEM((B_HEADS, tq, 1), F32),
                        pltpu.VMEM((B_HEADS, tq, KV_LORA), F32)],
        compiler_params=_params("parallel", "arbitrary"),
        name="mla_prompt",
    )(qcat, kcat, wuv_pad)


PAGES_PER_STEP = 16


def _mla_s_kernel(n_new, pt_ref, ql_ref, qp_ref, cn_ref, kn_ref, *rest):
    ck_refs = rest[:PAGES_PER_STEP]
    kp_refs = rest[PAGES_PER_STEP:2 * PAGES_PER_STEP]
    wuv_ref, hm_ref, o_ref, m_ref, l_ref, acc_ref = rest[2 * PAGES_PER_STEP:]
    step = pl.program_id(1)
    ql = ql_ref[...]
    qp = qp_ref[...]
    n_rows = ql.shape[0]

    @pl.when(step == 0)
    def _():
        cn = cn_ref[...].astype(BF16)
        kn = kn_ref[...].astype(BF16)
        s = (_dot_nt(ql, cn) + _dot_nt(qp, kn)) * SM_SCALE
        tok = lax.broadcasted_iota(jnp.int32, s.shape, 0) // B_HEADS
        col = lax.broadcasted_iota(jnp.int32, s.shape, 1)
        s = jnp.where(col <= tok, s, NEG_INF)
        m = jnp.max(s, axis=-1, keepdims=True)
        p = jnp.exp(s - m)
        m_ref[...] = m
        l_ref[...] = jnp.sum(p, axis=-1, keepdims=True)
        acc_ref[...] = _dot(p.astype(BF16), cn)

    cb = jnp.concatenate([r[...].astype(BF16) for r in ck_refs], axis=0)
    kb = jnp.concatenate([r[...].astype(BF16) for r in kp_refs], axis=0)
    s = (_dot_nt(ql, cb) + _dot_nt(qp, kb)) * SM_SCALE
    m = m_ref[...]
    m_new = jnp.maximum(m, jnp.max(s, axis=-1, keepdims=True))
    alpha = jnp.exp(m - m_new)
    p = jnp.exp(s - m_new)
    m_ref[...] = m_new
    l_ref[...] = alpha * l_ref[...] + jnp.sum(p, axis=-1, keepdims=True)
    acc_ref[...] = alpha * acc_ref[...] + _dot(p.astype(BF16), cb)

    @pl.when(step == pl.num_programs(1) - 1)
    def _():
        o = (acc_ref[...] / l_ref[...]).astype(BF16)
        full = _dot(o, wuv_ref[...])
        full = full.reshape(n_new, B_HEADS, B_HEADS * V_HD) * hm_ref[...]
        o_ref[...] = jnp.sum(full, axis=1)


def _mla_sample(q_lat, q_pe, c_new, k_new, cache_ckv, cache_kpe, page_table, wuv_all, head_mask,
                layer, n_new):
    n_seq, n_pages = page_table.shape
    n_steps = n_pages // PAGES_PER_STEP
    n_rows = q_lat.shape[1]

    def page_spec(width, k):
        return pl.BlockSpec((None, None, PAGE_SIZE, width),
                            lambda b, s, pt: (layer, pt[b, s * PAGES_PER_STEP + k], 0, 0))

    in_specs = [pl.BlockSpec((None, n_rows, KV_LORA), lambda b, s, pt: (b, 0, 0)),
                pl.BlockSpec((None, n_rows, QK_ROPE), lambda b, s, pt: (b, 0, 0)),
                pl.BlockSpec((None, SUBLANES, KV_LORA), lambda b, s, pt: (b, 0, 0)),
                pl.BlockSpec((None, SUBLANES, QK_ROPE), lambda b, s, pt: (b, 0, 0))]
    in_specs += [page_spec(KV_LORA, k) for k in range(PAGES_PER_STEP)]
    in_specs += [page_spec(QK_ROPE, k) for k in range(PAGES_PER_STEP)]
    in_specs += [pl.BlockSpec(wuv_all.shape, lambda b, s, pt: (0, 0)),
                 pl.BlockSpec(head_mask.shape, lambda b, s, pt: (0, 0))]
    grid_spec = pltpu.PrefetchScalarGridSpec(
        num_scalar_prefetch=1,
        grid=(n_seq, n_steps),
        in_specs=in_specs,
        out_specs=pl.BlockSpec((None, n_new, B_HEADS * V_HD), lambda b, s, pt: (b, 0, 0)),
        scratch_shapes=[pltpu.VMEM((n_rows, 1), F32), pltpu.VMEM((n_rows, 1), F32),
                        pltpu.VMEM((n_rows, KV_LORA), F32)],
    )
    return pl.pallas_call(
        functools.partial(_mla_s_kernel, n_new),
        grid_spec=grid_spec,
        out_shape=jax.ShapeDtypeStruct((n_seq, n_new, B_HEADS * V_HD), F32),
        compiler_params=_params("parallel", "arbitrary"),
        name="mla_sample",
    )(page_table, q_lat, q_pe, c_new, k_new, *([cache_ckv] * PAGES_PER_STEP),
      *([cache_kpe] * PAGES_PER_STEP), wuv_all, head_mask)


def _kc_kernel(x_ref, w_ref, gb_ref, q_out, k_out, v_out, o_out, g_out):
    proj = _dot(x_ref[...].astype(BF16), w_ref[...])
    q_out[...] = (proj[:, :OFF_MK] * (C_DK ** -0.5)).astype(BF16)
    k_out[...] = proj[:, OFF_MK:OFF_MV].astype(BF16)
    v_out[...] = proj[:, OFF_MV:OFF_MO].astype(BF16)
    o_out[...] = jax.nn.sigmoid(proj[:, OFF_MO:OFF_MG])
    pre = proj[:, OFF_MG:] + gb_ref[...]
    cap = GATE_CAP * jnp.tanh(pre / GATE_CAP)
    lane = lax.broadcasted_iota(jnp.int32, cap.shape, 1)
    g_out[...] = jnp.where(lane < C_HEADS, cap, jax.nn.log_sigmoid(cap))


def _layer1_prep(x, w_in, gate_bias, tm):
    rows = x.shape[0]
    widths = [OFF_MK, OFF_MK, MIX_O, MIX_O, LANES]
    dtypes = [BF16, BF16, BF16, F32, F32]
    return pl.pallas_call(
        _kc_kernel,
        grid=(rows // tm,),
        in_specs=[pl.BlockSpec((tm, D_MODEL), lambda i: (i, 0)), _full(w_in.shape),
                  _full(gate_bias.shape)],
        out_specs=[pl.BlockSpec((tm, wd), lambda i: (i, 0)) for wd in widths],
        out_shape=[jax.ShapeDtypeStruct((rows, wd), dt) for wd, dt in zip(widths, dtypes)],
        compiler_params=_params("parallel"),
        name="layer1_prep",
    )(x, w_in, gate_bias)


def _mlstm_kernel(chunk, q_ref, k_ref, v_ref, g_ref, c0_ref, n0_ref, m0_ref, h_ref, co_ref,
                  no_ref, mo_ref, c_s, n_s, m_s):
    ci = pl.program_id(1)

    @pl.when(ci == 0)
    def _():
        c_s[...] = c0_ref[...]
        n_s[...] = n0_ref[...]
        m_s[...] = m0_ref[...]

    g = g_ref[...]
    row = lax.broadcasted_iota(jnp.int32, g.shape, 0)
    lane = lax.broadcasted_iota(jnp.int32, g.shape, 1)
    b = g
    shift = 1
    while shift < chunk:
        b = b + jnp.where(row >= shift, pltpu.roll(b, shift, 0), 0.0)
        shift *= 2
    g_t = g.T
    b_t = b.T
    rr = lax.broadcasted_iota(jnp.int32, (chunk, chunk), 0)
    cc = lax.broadcasted_iota(jnp.int32, (chunk, chunk), 1)
    causal = cc <= rr
    for h in range(C_HEADS):
        b_col = jnp.sum(jnp.where(lane == C_HEADS + h, b, 0.0), axis=-1, keepdims=True)
        i_col = jnp.sum(jnp.where(lane == h, g, 0.0), axis=-1, keepdims=True)
        b_row = b_t[C_HEADS + h:C_HEADS + h + 1, :]
        i_row = g_t[h:h + 1, :]
        m_prev = m_s[h:h + 1, 0:1]
        d = jnp.where(causal, b_col - b_row + i_row, NEG_INF)
        gg = b_col + m_prev
        mt = jnp.maximum(gg, jnp.max(d, axis=-1, keepdims=True))
        w_inter = jnp.exp(gg - mt)
        qh = q_ref[:, h * HEAD_PAD:(h + 1) * HEAD_PAD]
        kh = k_ref[:, h * HEAD_PAD:(h + 1) * HEAD_PAD]
        vh = v_ref[:, h * C_DV:(h + 1) * C_DV]
        c_prev = c_s[h]
        n_prev = n_s[h:h + 1, :]
        att = jnp.exp(d - mt) * _dot_nt(qh, kh)
        num = w_inter * _dot_nt(qh, c_prev.astype(BF16)) + _dot(att.astype(BF16), vh)
        qn = jnp.sum(qh.astype(F32) * n_prev, axis=-1, keepdims=True)
        den = w_inter * qn + jnp.sum(att, axis=-1, keepdims=True)
        h_ref[:, h * C_DV:(h + 1) * C_DV] = num / jnp.maximum(jnp.abs(den), jnp.exp(-mt))
        m_new = mt[chunk - 1:chunk, :]
        b_last = b_col[chunk - 1:chunk, :]
        w_state = jnp.exp(b_last - b_col + i_col - m_new)
        dec = jnp.exp(b_last + m_prev - m_new)
        wv = (w_state * vh.astype(F32)).astype(BF16)
        c_s[h] = dec * c_prev + _dot_tn(wv, kh)
        n_s[h:h + 1, :] = dec * n_prev + jnp.sum(w_state * kh.astype(F32), axis=0, keepdims=True)
        m_s[h:h + 1, :] = jnp.broadcast_to(m_new, (1, LANES))

    @pl.when(ci == pl.num_programs(1) - 1)
    def _():
        co_ref[...] = c_s[...]
        no_ref[...] = n_s[...]
        mo_ref[...] = m_s[...]


def _mlstm(q, k, v, gates, c0, n0, m0, bsz, t_len, chunk):
    nc = t_len // chunk
    seq = lambda wd: pl.BlockSpec((chunk, wd), lambda b, c: (b * nc + c, 0))
    c_spec = pl.BlockSpec((None, C_HEADS, C_DV, HEAD_PAD), lambda b, c: (b, 0, 0, 0))
    v_spec = pl.BlockSpec((None, C_HEADS, LANES), lambda b, c: (b, 0, 0))
    return pl.pallas_call(
        functools.partial(_mlstm_kernel, chunk),
        grid=(bsz, nc),
        in_specs=[seq(OFF_MK), seq(OFF_MK), seq(MIX_O), seq(LANES), c_spec, v_spec, v_spec],
        out_specs=[seq(MIX_O), c_spec, v_spec, v_spec],
        out_shape=[jax.ShapeDtypeStruct((bsz * t_len, MIX_O), F32),
                   jax.ShapeDtypeStruct(c0.shape, F32),
                   jax.ShapeDtypeStruct(n0.shape, F32),
                   jax.ShapeDtypeStruct(m0.shape, F32)],
        scratch_shapes=[pltpu.VMEM((C_HEADS, C_DV, HEAD_PAD), F32),
                        pltpu.VMEM((C_HEADS, LANES), F32), pltpu.VMEM((C_HEADS, LANES), F32)],
        compiler_params=_params("parallel", "arbitrary"),
        name="mlstm",
    )(q, k, v, gates, c0, n0, m0)


def _kd_kernel(h_ref, o_ref, x_ref, mh_ref, wo_ref, ng_ref, nb_ref, out_ref):
    parts = []
    for h in range(C_HEADS):
        hh = h_ref[:, h * C_DV:(h + 1) * C_DV]
        parts.append(hh * lax.rsqrt(jnp.mean(hh * hh, axis=-1, keepdims=True) + 1e-6))
    hn = jnp.concatenate(parts, axis=-1) * mh_ref[...]
    y = _dot((hn * o_ref[...]).astype(BF16), wo_ref[...])
    out_ref[...] = _layer_norm_res(x_ref[...], y, ng_ref[...], nb_ref[...])


def _layer1_out(h, o_sig, x, mh_norm, w_out, ln_g, ln_b, tm):
    rows = x.shape[0]
    full = pl.BlockSpec((tm, D_MODEL), lambda i: (i, 0))
    consts = [mh_norm, w_out, ln_g, ln_b]
    return pl.pallas_call(
        _kd_kernel,
        grid=(rows // tm,),
        in_specs=[full, full, full] + [_full(c.shape) for c in consts],
        out_specs=full,
        out_shape=jax.ShapeDtypeStruct((rows, D_MODEL), F32),
        compiler_params=_params("parallel"),
        name="layer1_out",
    )(h, o_sig, x, *consts)


def _top16(s_ref, val_ref, idx_ref):
    n = s_ref.shape[0]
    row = lax.broadcasted_iota(jnp.int32, s_ref.shape, 0)
    s = s_ref[...]
    for r in range(PEER_TOPK):
        m = jnp.max(s, axis=0, keepdims=True)
        am = jnp.min(jnp.where(s == m, row, n), axis=0, keepdims=True)
        val_ref[r:r + 1, :] = m
        idx_ref[r:r + 1, :] = am
        s = jnp.where(row == am, NEG_INF, s)


def _pick_rank(src_ref, rank):
    out = jnp.zeros(rank.shape, jnp.int32)
    for r in range(PEER_TOPK):
        out = jnp.where(rank == r, src_ref[r:r + 1, :], out)
    return out


CAND_COUNT = [PEER_TOPK // (i + 1) for i in range(PEER_TOPK)]
CAND_START = [sum(CAND_COUNT[:i]) for i in range(PEER_TOPK + 1)]
CAND_ROWS = -(-CAND_START[PEER_TOPK] // SUBLANES) * SUBLANES


def _peer_q_kernel(tb, x_ref, wq_ref, k1_ref, k2_ref, i1_out, i2_out, g_out,
                   s_ref, cand_ref, t1_ref, j1_ref, t2_ref, j2_ref, tc_ref, jc_ref,
                   e1_ref, e2_ref, gt_ref):
    q = _dot(x_ref[...].astype(BF16), wq_ref[...])
    for h in range(PEER_HEADS):
        q1 = q[:, h * PEER_QDIM:h * PEER_QDIM + PEER_HALF].astype(BF16)
        q2 = q[:, h * PEER_QDIM + PEER_HALF:(h + 1) * PEER_QDIM].astype(BF16)
        s_ref[...] = _dot_nt(k1_ref[...], q1)
        _top16(s_ref, t1_ref, j1_ref)
        s_ref[...] = _dot_nt(k2_ref[...], q2)
        _top16(s_ref, t2_ref, j2_ref)
        for i in range(PEER_TOPK):
            lo, n = CAND_START[i], CAND_COUNT[i]
            cand_ref[lo:lo + n, :] = t1_ref[i:i + 1, :] + t2_ref[0:n, :]
        cand_ref[CAND_START[PEER_TOPK]:, :] = jnp.full(
            (CAND_ROWS - CAND_START[PEER_TOPK], tb), NEG_INF, F32)
        _top16(cand_ref, tc_ref, jc_ref)
        pos = jc_ref[...]
        rank1 = jnp.zeros_like(pos)
        start = jnp.zeros_like(pos)
        for i in range(1, PEER_TOPK):
            past = pos >= CAND_START[i]
            rank1 = rank1 + past.astype(jnp.int32)
            start = jnp.where(past, CAND_START[i], start)
        rows = slice(h * PEER_TOPK, (h + 1) * PEER_TOPK)
        e1_ref[rows, :] = _pick_rank(j1_ref, rank1).astype(F32)
        e2_ref[rows, :] = _pick_rank(j2_ref, pos - start).astype(F32)
        top = tc_ref[...]
        e = jnp.exp(top - top[0:1, :])
        gt_ref[rows, :] = e / jnp.sum(e, axis=0, keepdims=True)
    i1_out[...] = e1_ref[...].T.astype(jnp.int32)
    i2_out[...] = e2_ref[...].T.astype(jnp.int32)
    g_out[...] = gt_ref[...].T


def _peer_select(x, wq, k1, k2, tb):
    rows = x.shape[0]
    out = pl.BlockSpec((tb, PEER_PAIRS), lambda i: (i, 0))
    small = lambda dt: pltpu.VMEM((PEER_TOPK, tb), dt)
    return pl.pallas_call(
        functools.partial(_peer_q_kernel, tb),
        grid=(rows // tb,),
        in_specs=[pl.BlockSpec((tb, D_MODEL), lambda i: (i, 0)), _full(wq.shape), _full(k1.shape),
                  _full(k2.shape)],
        out_specs=[out, out, out],
        out_shape=[jax.ShapeDtypeStruct((rows, PEER_PAIRS), jnp.int32),
                   jax.ShapeDtypeStruct((rows, PEER_PAIRS), jnp.int32),
                   jax.ShapeDtypeStruct((rows, PEER_PAIRS), F32)],
        scratch_shapes=[pltpu.VMEM((N_KEYS, tb), F32),
                        pltpu.VMEM((CAND_ROWS, tb), F32),
                        small(F32), small(jnp.int32), small(F32), small(jnp.int32),
                        small(F32), small(jnp.int32),
                        pltpu.VMEM((PEER_PAIRS, tb), F32), pltpu.VMEM((PEER_PAIRS, tb), F32),
                        pltpu.VMEM((PEER_PAIRS, tb), F32)],
        compiler_params=_params("parallel"),
        name="peer_select",
    )(x, wq, k1, k2)


PEER_SPLIT = 2
TOKEN_UNROLL = 8


def _peer_u_kernel(tb, n_i1, x_ref, u_ref, i1_ref, i2_ref, g_ref, a_ref, hs_ref):
    part = pl.program_id(0)
    xb = x_ref[...].astype(BF16)
    for pp in range(n_i1 // 2):
        h2 = _dot_nt(xb, u_ref[pp * 2 * N_KEYS:(pp + 1) * 2 * N_KEYS, :])
        hs_ref[(2 * pp) * tb:(2 * pp + 1) * tb, :] = h2[:, :N_KEYS]
        hs_ref[(2 * pp + 1) * tb:(2 * pp + 2) * tb, :] = h2[:, N_KEYS:]
    sub1 = lax.broadcasted_iota(jnp.int32, (n_i1, PEER_PAIRS), 0) + part * n_i1
    sub2 = lax.broadcasted_iota(jnp.int32, (N_KEYS, PEER_PAIRS), 0)

    def token(t, carry):
        i1 = i1_ref[pl.ds(t, 1), :]
        i2 = i2_ref[pl.ds(t, 1), :]
        gate = g_ref[pl.ds(t, 1), :]
        sel1 = jnp.where(sub1 == i1, gate, 0.0).astype(BF16)
        sel2 = jnp.where(sub2 == i2, 1.0, 0.0).astype(BF16)
        gmat = _dot_nt(sel1, sel2)
        rows = pl.ds(t, n_i1, stride=tb)
        ht = hs_ref[rows, :]
        act = 0.5 * ht * (1.0 + lax.erf(ht * (2.0 ** -0.5)))
        hs_ref[rows, :] = act * gmat
        return carry

    lax.fori_loop(0, tb, token, 0, unroll=TOKEN_UNROLL)
    for i in range(n_i1):
        a_ref[:, i * N_KEYS:(i + 1) * N_KEYS] = hs_ref[i * tb:(i + 1) * tb, :].astype(BF16)


def _peer_act(x, u, i1, i2, gate, tb):
    rows = x.shape[0]
    n_i1 = N_KEYS // PEER_SPLIT
    n_exp = N_EXPERTS // PEER_SPLIT
    tok = lambda wd: pl.BlockSpec((tb, wd), lambda p, i: (i, 0))
    slab = pltpu.VMEM((n_i1 * tb, N_KEYS), F32)
    return pl.pallas_call(
        functools.partial(_peer_u_kernel, tb, n_i1),
        grid=(PEER_SPLIT, rows // tb),
        in_specs=[tok(D_MODEL),
                  pl.BlockSpec((n_exp, D_MODEL), lambda p, i: (p, 0),
                               pipeline_mode=pl.Buffered(1)),
                  tok(PEER_PAIRS), tok(PEER_PAIRS), tok(PEER_PAIRS)],
        out_specs=pl.BlockSpec((tb, n_exp), lambda p, i: (i, p)),
        out_shape=jax.ShapeDtypeStruct((rows, N_EXPERTS), BF16),
        scratch_shapes=[slab],
        compiler_params=_params("arbitrary", "arbitrary"),
        name="peer_act",
    )(x, u, i1, i2, gate)


def _peer_v_kernel(a_ref, v_ref, x_ref, ng_ref, nb_ref, o_ref, acc_ref):
    kk = pl.program_id(1)

    @pl.when(kk == 0)
    def _():
        acc_ref[...] = jnp.zeros_like(acc_ref)

    acc_ref[...] += _dot(a_ref[...], v_ref[...])

    @pl.when(kk == pl.num_programs(1) - 1)
    def _():
        o_ref[...] = _layer_norm_res(x_ref[...], acc_ref[...], ng_ref[...], nb_ref[...])


def _peer_out(a, v, x, ln_g, ln_b, tm, tk):
    rows = x.shape[0]
    return pl.pallas_call(
        _peer_v_kernel,
        grid=(rows // tm, N_EXPERTS // tk),
        in_specs=[pl.BlockSpec((tm, tk), lambda i, k: (i, k)),
                  pl.BlockSpec((tk, D_MODEL), lambda i, k: (k, 0)),
                  pl.BlockSpec((tm, D_MODEL), lambda i, k: (i, 0)),
                  pl.BlockSpec(ln_g.shape, lambda i, k: (0, 0)),
                  pl.BlockSpec(ln_b.shape, lambda i, k: (0, 0))],
        out_specs=pl.BlockSpec((tm, D_MODEL), lambda i, k: (i, 0)),
        out_shape=jax.ShapeDtypeStruct((rows, D_MODEL), F32),
        scratch_shapes=[pltpu.VMEM((tm, D_MODEL), F32)],
        compiler_params=_params("parallel", "arbitrary"),
        name="peer_out",
    )(a, v, x, ln_g, ln_b)


PEER_TB_SELECT = 256
PEER_TB_ACT = 256
PEER_TM_OUT, PEER_TK_OUT = 1024, 2048


def _peer(x, pw, ln_g, ln_b):
    rows = x.shape[0]
    i1, i2, gate = _peer_select(x, pw["wq"], pw["k1"], pw["k2"], min(PEER_TB_SELECT, rows))
    act = _peer_act(x, pw["u"], i1, i2, gate, min(PEER_TB_ACT, rows))
    return _peer_out(act, pw["v"], x, ln_g, ln_b, min(PEER_TM_OUT, rows), PEER_TK_OUT)


def _pad_cols(w, width):
    return jnp.pad(w, ((0, 0), (0, width - w.shape[1])))


def _row(v):
    return v.reshape(1, -1).astype(F32)


def _layer0_weights(w_in, mu, w0, w2, a0, a2, g2, kk, ka, rk, lnx_w, lnx_b, q_norm, w_uq, kv_norm,
                    w_uk, w_uv, w_out):
    c0 = 3 * A_W
    kpe0 = RWKV_COLS + Q_LORA + KV_LORA
    tile = lambda cols: jnp.tile(cols, (1, B_HEADS))
    w_in_p = jnp.concatenate([
        w_in[:, :c0],
        _pad_cols(w_in[:, c0:c0 + LORA_W], LANES),
        _pad_cols(w_in[:, c0 + LORA_W:c0 + LORA_W + LORA_A], LANES),
        w_in[:, c0 + LORA_W + LORA_A:RWKV_COLS],
        w_in[:, RWKV_COLS:kpe0],
        tile(w_in[:, kpe0:kpe0 + ROPE_HALF]),
        tile(w_in[:, kpe0 + ROPE_HALF:kpe0 + QK_ROPE]),
    ], axis=1).astype(BF16)
    mu_p = jnp.concatenate([mu[:c0], jnp.pad(mu[c0:c0 + LORA_W], (0, LANES - LORA_W)),
                            jnp.pad(mu[c0 + LORA_W:c0 + LORA_W + LORA_A], (0, LANES - LORA_A)),
                            mu[c0 + LORA_W + LORA_A:]])
    wq3 = w_uq.reshape(Q_LORA, B_HEADS, QK_NOPE + QK_ROPE)
    nope = jnp.pad(wq3[:, :, :QK_NOPE], ((0, 0), (0, 0), (0, LANES - QK_NOPE)))
    w_uq_p = jnp.concatenate([
        nope.reshape(Q_LORA, B_HEADS * LANES),
        wq3[:, :, QK_NOPE:QK_NOPE + ROPE_HALF].reshape(Q_LORA, LANES),
        wq3[:, :, QK_NOPE + ROPE_HALF:].reshape(Q_LORA, LANES),
    ], axis=1).astype(BF16)
    w_uk_t = jnp.pad(jnp.transpose(w_uk, (1, 2, 0)), ((0, 0), (0, LANES - QK_NOPE), (0, 0)))
    heads = np.arange(B_HEADS)
    blk = np.zeros((B_HEADS, KV_LORA, B_HEADS * V_HD), np.float32)
    hm = np.zeros((B_HEADS, B_HEADS * V_HD), np.float32)
    pem = np.zeros((B_HEADS, 2 * LANES), np.float32)
    for h in heads:
        blk[h, :, h * V_HD:(h + 1) * V_HD] = 1.0
        hm[h, h * V_HD:(h + 1) * V_HD] = 1.0
        pem[h, h * ROPE_HALF:(h + 1) * ROPE_HALF] = 1.0
        pem[h, LANES + h * ROPE_HALF:LANES + (h + 1) * ROPE_HALF] = 1.0
    wuv_all = w_uv.reshape(KV_LORA, B_HEADS * V_HD)
    seg = np.kron(np.eye(A_HEADS, dtype=np.float32), np.ones((A_HD, A_HD), np.float32))
    return {
        "w_in": w_in_p, "mu": _row(mu_p), "w0": _row(w0),
        "w2": jnp.pad(w2, ((0, LANES - LORA_W), (0, 0))).astype(BF16), "a0": _row(a0),
        "a2": jnp.pad(a2, ((0, LANES - LORA_A), (0, 0))).astype(BF16), "g2": g2.astype(BF16),
        "kk": _row(kk), "ka": _row(ka), "rk": _row(rk), "lnx_w": _row(lnx_w), "lnx_b": _row(lnx_b),
        "q_norm": _row(q_norm), "w_uq": w_uq_p, "kv_norm": _row(kv_norm),
        "w_uk": w_uk_t.astype(BF16),
        "wuv_pad": (wuv_all[None] * jnp.asarray(blk)).astype(BF16),
        "wuv_all": wuv_all.astype(BF16), "head_mask": jnp.asarray(hm),
        "pe_mask": jnp.asarray(pem), "seg": jnp.asarray(seg, dtype=BF16),
        "w_out": w_out.astype(BF16),
    }


def _layer1_weights(w_in, b_i, b_f, mh_norm, w_out):
    hk = C_HEADS * C_DK
    padh = lambda w: jnp.pad(w.reshape(D_MODEL, C_HEADS, C_DK),
                             ((0, 0), (0, 0), (0, HEAD_PAD - C_DK))).reshape(D_MODEL, OFF_MK)
    w_in_p = jnp.concatenate([
        padh(w_in[:, :hk]), padh(w_in[:, hk:2 * hk]), w_in[:, 2 * hk:2 * hk + 2 * MIX_O],
        _pad_cols(w_in[:, 2 * hk + 2 * MIX_O:], LANES),
    ], axis=1).astype(BF16)
    gate_bias = jnp.pad(jnp.concatenate([b_i, b_f]), (0, LANES - 2 * C_HEADS)).reshape(1, LANES)
    return {"w_in": w_in_p, "gate_bias": gate_bias.astype(F32), "mh_norm": _row(mh_norm),
            "w_out": w_out.astype(BF16)}


def _rope_tables(pos):
    inv = ROPE_BASE ** (-jnp.arange(0, QK_ROPE, 2, dtype=F32) / QK_ROPE)
    ang = pos.astype(F32)[:, None] * inv[None, :]
    return jnp.tile(jnp.cos(ang), (1, B_HEADS)), jnp.tile(jnp.sin(ang), (1, B_HEADS))


def _unpack_kpe(kpe_rot, bsz, t_len):
    return jnp.concatenate([kpe_rot[:, :ROPE_HALF], kpe_rot[:, LANES:LANES + ROPE_HALF]],
                           axis=-1).reshape(bsz, t_len, QK_ROPE)


def kernel(x_prompt, x_sample, cache_ckv, cache_kpe, page_table, state_shift, state_wkv, state_mlstm_c, state_mlstm_n, state_mlstm_m, w_in_e, mu_e, w0_e, w2_e, a0_e, a2_e, g2_e, kk_e, ka_e, rk_e, lnx_w_e, lnx_b_e, q_norm_e, w_uq_e, kv_norm_e, w_uk_e, w_uv_e, w_out_e, w_in_o, b_i_o, b_f_o, mh_norm_o, w_out_o, peer_wq, peer_k1, peer_k2, peer_u, peer_v, ln1_g, ln1_b, ln2_g, ln2_b):
    bp, tp, _ = x_prompt.shape
    bs, ts, _ = x_sample.shape
    past_len = page_table.shape[1] * PAGE_SIZE
    xp = x_prompt.reshape(bp * tp, D_MODEL)
    xs = x_sample.reshape(bs * ts, D_MODEL)
    rows_s = bs * ts
    tm_s = min(256, rows_s)
    tm_p = 256

    e = 0
    w0 = _layer0_weights(w_in_e[e], mu_e[e], w0_e[e], w2_e[e], a0_e[e], a2_e[e], g2_e[e], kk_e[e],
                         ka_e[e], rk_e[e], lnx_w_e[e], lnx_b_e[e], q_norm_e[e], w_uq_e[e],
                         kv_norm_e[e], w_uk_e[e], w_uv_e[e], w_out_e[e])
    g1, b1 = _row(ln1_g[0]), _row(ln1_b[0])
    cos_p, sin_p = _rope_tables(jnp.arange(tp))
    cos_s, sin_s = _rope_tables(past_len + jnp.arange(ts))
    cos_s = jnp.tile(cos_s, (tm_s // ts, 1))
    sin_s = jnp.tile(sin_s, (tm_s // ts, 1))

    (r, w, k, v, kk, kka, g, bonus, qcat, _, c_p, kpe_p, kcat) = _layer0_prep(
        xp, None, w0, cos_p, sin_p, bp, tp // tm_p, tm_p, tp)
    ch = lambda a: _to_chains(a, bp, tp)
    s0_p = jnp.zeros((A_HD, A_HD, bp * A_HEADS), F32)
    y, s_p = _wkv_scan(ch(r), ch(w), ch(k), ch(v), ch(kk), ch(kka), s0_p, 32)
    y = _from_chains(y, bp, tp)
    out_b = _mla_prompt(qcat, kcat, w0["wuv_pad"], bp, tp, 256)
    xp1 = _layer0_out(y, g, bonus, out_b, xp, w0, g1, b1, tm_p)
    wkv_p = jnp.transpose(s_p.reshape(A_HD, A_HD, bp, A_HEADS), (2, 3, 1, 0))

    start = _mm(state_shift[e], w0["w_in"][:, :RW_PAD], bs)
    start = jnp.repeat(start, ts, axis=0)
    (r, w, k, v, kk, kka, g, bonus, qcat, qpe, c_s, kpe_s, _) = _layer0_prep(
        xs, start, w0, cos_s, sin_s, rows_s // tm_s, 1, tm_s, ts)
    ch = lambda a: _to_chains(a, bs, ts)
    s0_s = jnp.transpose(state_wkv[e], (3, 2, 0, 1)).reshape(A_HD, A_HD, bs * A_HEADS)
    y, s_s = _wkv_scan(ch(r), ch(w), ch(k), ch(v), ch(kk), ch(kka), s0_s, ts)
    y = _from_chains(y, bs, ts)
    q_lat = qcat.reshape(bs, ts * B_HEADS, QCAT)[:, :, :KV_LORA]
    q_pe = jnp.transpose(qpe.reshape(bs, ts, 2, B_HEADS, ROPE_HALF), (0, 1, 3, 2, 4))
    q_pe = q_pe.reshape(bs, ts * B_HEADS, QK_ROPE).astype(BF16)
    kpe_new = _unpack_kpe(kpe_s, bs, ts)
    pad_new = lambda a: jnp.pad(a, ((0, 0), (0, SUBLANES - ts), (0, 0)))
    out_b = _mla_sample(q_lat, q_pe, pad_new(c_s.reshape(bs, ts, KV_LORA)), pad_new(kpe_new),
                        cache_ckv, cache_kpe, page_table, w0["wuv_all"], w0["head_mask"], e, ts)
    xs1 = _layer0_out(y, g, bonus, out_b.reshape(rows_s, B_HEADS * V_HD), xs, w0, g1, b1, tm_s)
    wkv_s = jnp.transpose(s_s.reshape(A_HD, A_HD, bs, A_HEADS), (2, 3, 1, 0))

    new_ckv_prompt = c_p.reshape(1, bp, tp, KV_LORA)
    new_kpe_prompt = _unpack_kpe(kpe_p, bp, tp)[None]
    new_ckv_sample = c_s.reshape(1, bs, ts, KV_LORA)
    new_kpe_sample = kpe_new[None]
    new_shift_prompt = x_prompt[:, -1][None]
    new_shift_sample = x_sample[:, -1][None]

    def peer_weights(l):
        return {"wq": peer_wq[l].astype(BF16), "k1": peer_k1[l].astype(BF16),
                "k2": peer_k2[l].astype(BF16), "u": peer_u[l].astype(BF16),
                "v": peer_v[l].astype(BF16)}

    pw = peer_weights(0)
    g2n, b2n = _row(ln2_g[0]), _row(ln2_b[0])
    xp2 = _peer(xp1, pw, g2n, b2n)
    xs2 = _peer(xs1, pw, g2n, b2n)

    od = 0
    w1 = _layer1_weights(w_in_o[od], b_i_o[od], b_f_o[od], mh_norm_o[od], w_out_o[od])
    g1, b1 = _row(ln1_g[1]), _row(ln1_b[1])
    pad_dk = lambda a: jnp.pad(a, [(0, 0)] * (a.ndim - 1) + [(0, HEAD_PAD - C_DK)])

    q, k, v, o_sig, gates = _layer1_prep(xp2, w1["w_in"], w1["gate_bias"], tm_p)
    zc = jnp.zeros((bp, C_HEADS, C_DV, HEAD_PAD), F32)
    zn = jnp.zeros((bp, C_HEADS, LANES), F32)
    h, c_p1, n_p1, m_p1 = _mlstm(q, k, v, gates, zc, zn, zn, bp, tp, MLSTM_CHUNK)
    xp3 = _layer1_out(h, o_sig, xp2, w1["mh_norm"], w1["w_out"], g1, b1, tm_p)

    q, k, v, o_sig, gates = _layer1_prep(xs2, w1["w_in"], w1["gate_bias"], tm_s)
    lp = MLSTM_CHUNK
    pad_t = lambda a: jnp.pad(a.reshape(bs, ts, -1), ((0, 0), (0, lp - ts), (0, 0))).reshape(
        bs * lp, -1)
    lane = jnp.arange(LANES)
    gate_fill = jnp.where(lane < C_HEADS, GATE_PAD, 0.0).astype(F32)
    gates_p = jnp.concatenate(
        [gates.reshape(bs, ts, LANES), jnp.broadcast_to(gate_fill, (bs, lp - ts, LANES))],
        axis=1).reshape(bs * lp, LANES)
    m0 = jnp.broadcast_to(state_mlstm_m[od][:, :, None], (bs, C_HEADS, LANES))
    h, c_s1, n_s1, m_s1 = _mlstm(pad_t(q), pad_t(k), pad_t(v), gates_p, pad_dk(state_mlstm_c[od]),
                                 pad_dk(state_mlstm_n[od]), m0, bs, lp, lp)
    h = h.reshape(bs, lp, MIX_O)[:, :ts].reshape(rows_s, MIX_O)
    xs3 = _layer1_out(h, o_sig, xs2, w1["mh_norm"], w1["w_out"], g1, b1, tm_s)

    pw = peer_weights(1)
    g2n, b2n = _row(ln2_g[1]), _row(ln2_b[1])
    xp4 = _peer(xp3, pw, g2n, b2n)
    xs4 = _peer(xs3, pw, g2n, b2n)

    return (xp4.reshape(bp, tp, D_MODEL), xs4.reshape(bs, ts, D_MODEL),
            new_ckv_prompt, new_kpe_prompt, new_ckv_sample, new_kpe_sample,
            new_shift_prompt, new_shift_sample, wkv_p[None], wkv_s[None],
            c_p1[..., :C_DK][None], c_s1[..., :C_DK][None],
            n_p1[..., :C_DK][None], n_s1[..., :C_DK][None],
            m_p1[..., 0][None], m_s1[..., 0][None])
```

```python
import functools

import jax
import jax.numpy as jnp
import numpy as np
from jax import lax
from jax.experimental import pallas as pl
from jax.experimental.pallas import tpu as pltpu

F32 = jnp.float32
BF16 = jnp.bfloat16
NEG_INF = float("-inf")

LANES = 128
SUBLANES = 8
VMEM_LIMIT_BYTES = 56 * 1024 * 1024

D_MODEL = 1024
DEPTH = 2
PAGE_SIZE = 128
ALPHA = (2 * DEPTH) ** 0.25
LN_EPS = 1e-5
A_HEADS, A_HD = 8, 64
A_W = A_HEADS * A_HD
LORA_W, LORA_A, LORA_G = 64, 64, 128
RWKV_COLS = 3 * A_W + LORA_W + LORA_A + LORA_G
GN_EPS = 64e-5
B_HEADS, QK_NOPE, QK_ROPE, V_HD = 8, 64, 32, 64
Q_LORA, KV_LORA = 384, 256
SM_SCALE = (QK_NOPE + QK_ROPE) ** -0.5
ROPE_BASE = 10000.0
ROPE_HALF = QK_ROPE // 2
C_HEADS, C_DK, C_DV = 8, 64, 128
MIX_O = C_HEADS * C_DV
MLSTM_CHUNK = 64
GATE_CAP = 15.0
GATE_PAD = -1e30
N_KEYS = 128
N_EXPERTS = N_KEYS * N_KEYS
PEER_HEADS, PEER_QDIM, PEER_TOPK = 8, 256, 16
PEER_HALF = PEER_QDIM // 2
PEER_PAIRS = PEER_HEADS * PEER_TOPK

RW_PAD = 3 * A_W + 3 * LANES
OFF_CQ = RW_PAD
OFF_CKV = OFF_CQ + Q_LORA
OFF_KP1 = OFF_CKV + KV_LORA
OFF_KP2 = OFF_KP1 + LANES
IN_E_PAD = OFF_KP2 + LANES
Q_PAD = B_HEADS * LANES + 2 * LANES
QCAT = 2 * KV_LORA
HEAD_PAD = LANES
OFF_MK = C_HEADS * HEAD_PAD
OFF_MV = 2 * C_HEADS * HEAD_PAD
OFF_MO = OFF_MV + MIX_O
OFF_MG = OFF_MO + MIX_O
IN_O_PAD = OFF_MG + LANES


def _params(*sem):
    return pltpu.CompilerParams(dimension_semantics=sem, vmem_limit_bytes=VMEM_LIMIT_BYTES)


def _dot(a, b):
    return jnp.dot(a, b, preferred_element_type=F32)


def _dot_nt(a, b):
    return lax.dot_general(a, b, (((1,), (1,)), ((), ())), preferred_element_type=F32)


def _dot_tn(a, b):
    return lax.dot_general(a, b, (((0,), (0,)), ((), ())), preferred_element_type=F32)


def _seg_sum(x, seg):
    hi = x.astype(BF16)
    lo = (x - hi.astype(F32)).astype(BF16)
    return _dot(hi, seg) + _dot(lo, seg)


def _layer_norm_res(x, h, g, b):
    z = ALPHA * x + h
    mu = jnp.mean(z, axis=-1, keepdims=True)
    zc = z - mu
    var = jnp.mean(zc * zc, axis=-1, keepdims=True)
    return zc * lax.rsqrt(var + LN_EPS) * g + b


def _full(shape):
    nd = len(shape)
    return pl.BlockSpec(shape, lambda *_: (0,) * nd)


def _mm_kernel(x_ref, w_ref, o_ref):
    o_ref[...] = _dot(x_ref[...].astype(BF16), w_ref[...])


def _mm(x, w, tm):
    m, k = x.shape
    n = w.shape[1]
    return pl.pallas_call(
        _mm_kernel,
        grid=(m // tm,),
        in_specs=[pl.BlockSpec((tm, k), lambda i: (i, 0)), _full((k, n))],
        out_specs=pl.BlockSpec((tm, n), lambda i: (i, 0)),
        out_shape=jax.ShapeDtypeStruct((m, n), F32),
        compiler_params=_params("parallel"),
        name="mm",
    )(x, w)


def _ka_kernel(has_start, seq_len, tm, *refs):
    if has_start:
        x_ref, start_ref = refs[:2]
        refs = refs[2:]
    else:
        x_ref = refs[0]
        start_ref = None
        refs = refs[1:]
    (w_in_ref, mu_ref, w0_ref, w2_ref, a0_ref, a2_ref, g2_ref, kk_ref, ka_ref, rk_ref, qn_ref,
     wuq_ref, kvn_ref, wuk_ref, cos_ref, sin_ref, seg_ref, pem_ref,
     r_out, w_out, k_out, v_out, kk_out, kka_out, g_out, bonus_out, qcat_out, qpe_out, c_out,
     kpe_out, kcat_out, carry_ref) = refs
    t_blk = pl.program_id(1)
    proj = _dot(x_ref[...].astype(BF16), w_in_ref[...])
    p_rw = proj[:, :RW_PAD]
    row = lax.broadcasted_iota(jnp.int32, (tm, 1), 0)
    prev = pltpu.roll(p_rw, 1, 0)
    if has_start:
        prev = jnp.where(row % seq_len == 0, start_ref[...], prev)
    else:
        @pl.when(t_blk == 0)
        def _():
            carry_ref[...] = jnp.zeros_like(carry_ref)

        prev = jnp.where(row == 0, carry_ref[SUBLANES - 1:SUBLANES, :], prev)
        carry_ref[...] = p_rw[tm - SUBLANES:, :]
    p = p_rw + (prev - p_rw) * mu_ref[...]
    r = p[:, 0:A_W]
    k = p[:, A_W:2 * A_W]
    v = p[:, 2 * A_W:3 * A_W]
    lw = p[:, 3 * A_W:3 * A_W + LANES]
    la = p[:, 3 * A_W + LANES:3 * A_W + 2 * LANES]
    lg = p[:, 3 * A_W + 2 * LANES:RW_PAD]
    w_pre = w0_ref[...] + _dot(jnp.tanh(lw).astype(BF16), w2_ref[...])
    w_log = -jax.nn.softplus(-w_pre) - 0.5
    decay = jnp.exp(-jnp.exp(w_log))
    a = jax.nn.sigmoid(a0_ref[...] + _dot(la.astype(BF16), a2_ref[...]))
    g = _dot(jax.nn.sigmoid(lg).astype(BF16), g2_ref[...])
    seg = seg_ref[...]
    kk = k * kk_ref[...]
    kk = kk / jnp.maximum(jnp.sqrt(_seg_sum(kk * kk, seg)), 1e-12)
    k = k * (1.0 + (a - 1.0) * ka_ref[...])
    r_out[...] = r
    w_out[...] = decay
    k_out[...] = k
    v_out[...] = v
    kk_out[...] = kk
    kka_out[...] = kk * a
    g_out[...] = g
    bonus_out[...] = _seg_sum(r * k * rk_ref[...], seg) * v
    cq = proj[:, OFF_CQ:OFF_CKV]
    ckv = proj[:, OFF_CKV:OFF_KP1]
    kp1 = proj[:, OFF_KP1:OFF_KP2]
    kp2 = proj[:, OFF_KP2:IN_E_PAD]
    cos = cos_ref[...]
    sin = sin_ref[...]
    cqn = cq * lax.rsqrt(jnp.mean(cq * cq, axis=-1, keepdims=True) + 1e-6) * qn_ref[...]
    q = _dot(cqn.astype(BF16), wuq_ref[...])
    x1 = q[:, B_HEADS * LANES:B_HEADS * LANES + LANES]
    x2 = q[:, B_HEADS * LANES + LANES:Q_PAD]
    qpe = jnp.concatenate([x1 * cos - x2 * sin, x1 * sin + x2 * cos], axis=-1)
    qpe_out[...] = qpe
    for h in range(B_HEADS):
        q_lat = _dot(q[:, h * LANES:(h + 1) * LANES].astype(BF16), wuk_ref[h])
        qcat_out[:, h * QCAT:h * QCAT + KV_LORA] = q_lat.astype(BF16)
        qcat_out[:, h * QCAT + KV_LORA:(h + 1) * QCAT] = (qpe * pem_ref[h:h + 1, :]).astype(BF16)
    c = ckv * lax.rsqrt(jnp.mean(ckv * ckv, axis=-1, keepdims=True) + 1e-6) * kvn_ref[...]
    kpe = jnp.concatenate([kp1 * cos - kp2 * sin, kp1 * sin + kp2 * cos], axis=-1)
    c_out[...] = c
    kpe_out[...] = kpe
    kcat_out[:, :KV_LORA] = c.astype(BF16)
    kcat_out[:, KV_LORA:] = kpe.astype(BF16)


def _layer0_prep(x, start, wts, cos, sin, n_seq_blocks, n_t_blocks, tm, seq_len):
    rows = x.shape[0]
    has_start = start is not None
    rmap = lambda b, t: (b * n_t_blocks + t, 0)
    tmap = lambda b, t: (t, 0)
    ins = [x] + ([start] if has_start else [])
    in_specs = [pl.BlockSpec((tm, D_MODEL), rmap)]
    if has_start:
        in_specs.append(pl.BlockSpec((tm, RW_PAD), rmap))
    w_names = ("w_in", "mu", "w0", "w2", "a0", "a2", "g2", "kk", "ka", "rk", "q_norm", "w_uq",
               "kv_norm", "w_uk")
    for nme in w_names:
        ins.append(wts[nme])
        in_specs.append(_full(wts[nme].shape))
    ins += [cos, sin, wts["seg"], wts["pe_mask"]]
    in_specs += [pl.BlockSpec((tm, LANES), tmap), pl.BlockSpec((tm, LANES), tmap),
                 _full(wts["seg"].shape), _full(wts["pe_mask"].shape)]
    widths = [A_W] * 8 + [B_HEADS * QCAT, 2 * LANES, KV_LORA, 2 * LANES, QCAT]
    dtypes = [F32] * 8 + [BF16, F32, F32, F32, BF16]
    out_shape = [jax.ShapeDtypeStruct((rows, wd), dt) for wd, dt in zip(widths, dtypes)]
    out_specs = [pl.BlockSpec((tm, wd), rmap) for wd in widths]
    return pl.pallas_call(
        functools.partial(_ka_kernel, has_start, seq_len, tm),
        grid=(n_seq_blocks, n_t_blocks),
        in_specs=in_specs,
        out_specs=out_specs,
        out_shape=out_shape,
        scratch_shapes=[pltpu.VMEM((SUBLANES, RW_PAD), F32)],
        compiler_params=_params("parallel", "arbitrary"),
        name="layer0_prep",
    )(*ins)


def _wkv_kernel(tch, r_ref, w_ref, k_ref, v_ref, kk_ref, kka_ref, s0_ref, y_ref, sout_ref,
                s_ref, sa_ref):
    tc = pl.program_id(1)

    @pl.when(tc == 0)
    def _():
        s_ref[...] = s0_ref[...]

    acc = jnp.zeros((A_HD, LANES), F32)
    for j in range(A_HD):
        acc = acc + s_ref[j] * kk_ref[0, pl.ds(j, 1), :]
    sa_ref[...] = -acc

    def step(t, carry):
        tn = jnp.minimum(t + 1, tch - 1)
        sa = sa_ref[...]
        v_t = v_ref[t]
        y = jnp.zeros((A_HD, LANES), F32)
        san = jnp.zeros((A_HD, LANES), F32)
        for j in range(A_HD):
            row = pl.ds(j, 1)
            sj = s_ref[j] * w_ref[t, row, :] + sa * kka_ref[t, row, :] + v_t * k_ref[t, row, :]
            s_ref[j] = sj
            y = y + sj * r_ref[t, row, :]
            san = san + sj * kk_ref[tn, row, :]
        y_ref[t] = y
        sa_ref[...] = -san
        return carry

    lax.fori_loop(0, tch, step, 0)

    @pl.when(tc == pl.num_programs(1) - 1)
    def _():
        sout_ref[...] = s_ref[...]


def _wkv_scan(r, w, k, v, kk, kka, s0, tch):
    t_len, _, n_chains = r.shape
    pad = (-n_chains) % LANES
    if pad:
        padc = lambda a: jnp.pad(a, ((0, 0), (0, 0), (0, pad)))
        y, s = _wkv_scan(*(padc(a) for a in (r, w, k, v, kk, kka, s0)), tch)
        return y[..., :n_chains], s[..., :n_chains]
    chains = n_chains
    seq = pl.BlockSpec((tch, A_HD, LANES), lambda c, t: (t, 0, c))
    st = pl.BlockSpec((A_HD, A_HD, LANES), lambda c, t: (0, 0, c))
    return pl.pallas_call(
        functools.partial(_wkv_kernel, tch),
        grid=(chains // LANES, t_len // tch),
        in_specs=[seq] * 6 + [st],
        out_specs=[seq, st],
        out_shape=[jax.ShapeDtypeStruct((t_len, A_HD, chains), F32),
                   jax.ShapeDtypeStruct((A_HD, A_HD, chains), F32)],
        scratch_shapes=[pltpu.VMEM((A_HD, A_HD, LANES), F32), pltpu.VMEM((A_HD, LANES), F32)],
        compiler_params=_params("parallel", "arbitrary"),
        name="wkv_scan",
    )(r, w, k, v, kk, kka, s0)


def _to_chains(x, bsz, t_len):
    x = x.reshape(bsz, t_len, A_HEADS, A_HD)
    return jnp.transpose(x, (1, 3, 0, 2)).reshape(t_len, A_HD, bsz * A_HEADS)


def _from_chains(y, bsz, t_len):
    y = y.reshape(t_len, A_HD, bsz, A_HEADS)
    return jnp.transpose(y, (2, 0, 3, 1)).reshape(bsz * t_len, A_W)


def _kb_kernel(y_ref, g_ref, bonus_ref, ob_ref, x_ref, lw_ref, lb_ref, seg_ref, wo_ref, ng_ref,
               nb_ref, o_ref):
    seg = seg_ref[...]
    y = y_ref[...]
    mu = _seg_sum(y, seg) * (1.0 / A_HD)
    yc = y - mu
    var = _seg_sum(yc * yc, seg) * (1.0 / A_HD)
    y = yc * lax.rsqrt(var + GN_EPS) * lw_ref[...] + lb_ref[...] + bonus_ref[...]
    out_a = (y * g_ref[...]).astype(BF16)
    h = _dot(out_a, wo_ref[:A_W, :]) + _dot(ob_ref[...].astype(BF16), wo_ref[A_W:, :])
    o_ref[...] = _layer_norm_res(x_ref[...], h, ng_ref[...], nb_ref[...])


def _layer0_out(y, g, bonus, out_b, x, wts, ln_g, ln_b, tm):
    rows = x.shape[0]
    half = pl.BlockSpec((tm, A_W), lambda i: (i, 0))
    full = pl.BlockSpec((tm, D_MODEL), lambda i: (i, 0))
    consts = [wts["lnx_w"], wts["lnx_b"], wts["seg"], wts["w_out"], ln_g, ln_b]
    return pl.pallas_call(
        _kb_kernel,
        grid=(rows // tm,),
        in_specs=[half, half, half, half, full] + [_full(c.shape) for c in consts],
        out_specs=full,
        out_shape=jax.ShapeDtypeStruct((rows, D_MODEL), F32),
        compiler_params=_params("parallel"),
        name="layer0_out",
    )(y, g, bonus, out_b, x, *consts)


def _mla_p_kernel(tq, q_ref, kc_ref, wuv_ref, o_ref, m_ref, l_ref, acc_ref):
    qi = pl.program_id(1)
    rr = lax.broadcasted_iota(jnp.int32, (tq, tq), 0)
    cc = lax.broadcasted_iota(jnp.int32, (tq, tq), 1)
    causal = cc <= rr
    m_ref[...] = jnp.full(m_ref.shape, NEG_INF, F32)
    l_ref[...] = jnp.zeros(l_ref.shape, F32)
    acc_ref[...] = jnp.zeros(acc_ref.shape, F32)

    def chunk(c, mask):
        kc = kc_ref[pl.ds(pl.multiple_of(c * tq, tq), tq), :]
        for h in range(B_HEADS):
            s = _dot_nt(q_ref[:, h * QCAT:(h + 1) * QCAT], kc) * SM_SCALE
            if mask:
                s = jnp.where(causal, s, NEG_INF)
            m = m_ref[h]
            m_new = jnp.maximum(m, jnp.max(s, axis=-1, keepdims=True))
            alpha = jnp.exp(m - m_new)
            p = jnp.exp(s - m_new)
            m_ref[h] = m_new
            l_ref[h] = alpha * l_ref[h] + jnp.sum(p, axis=-1, keepdims=True)
            acc_ref[h] = alpha * acc_ref[h] + _dot(p.astype(BF16), kc[:, :KV_LORA])

    def body(c, carry):
        chunk(c, False)
        return carry

    lax.fori_loop(0, qi, body, 0)
    chunk(qi, True)
    out = jnp.zeros((tq, B_HEADS * V_HD), F32)
    for h in range(B_HEADS):
        out = out + _dot((acc_ref[h] / l_ref[h]).astype(BF16), wuv_ref[h])
    o_ref[...] = out


def _mla_prompt(qcat, kcat, wuv_pad, bsz, t_len, tq):
    nq = t_len // tq
    return pl.pallas_call(
        functools.partial(_mla_p_kernel, tq),
        grid=(bsz, nq),
        in_specs=[pl.BlockSpec((tq, B_HEADS * QCAT), lambda b, i: (b * nq + i, 0)),
                  pl.BlockSpec((t_len, QCAT), lambda b, i: (b, 0)),
                  _full(wuv_pad.shape)],
        out_specs=pl.BlockSpec((tq, B_HEADS * V_HD), lambda b, i: (b * nq + i, 0)),
        out_shape=jax.ShapeDtypeStruct((bsz * t_len, B_HEADS * V_HD), F32),
        scratch_shapes=[pltpu.VMEM((B_HEADS, tq, 1), F32), pltpu.VMEM((B_HEADS, tq, 1), F32),
                        pltpu.VMEM((B_HEADS, tq, KV_LORA), F32)],
        compiler_params=_params("parallel", "arbitrary"),
        name="mla_prompt",
    )(qcat, kcat, wuv_pad)


PAGES_PER_STEP = 32


def _mla_s_kernel(n_new, pt_ref, ql_ref, qp_ref, cn_ref, kn_ref, *rest):
    ck_refs = rest[:PAGES_PER_STEP]
    kp_refs = rest[PAGES_PER_STEP:2 * PAGES_PER_STEP]
    wuv_ref, hm_ref, o_ref, m_ref, l_ref, acc_ref = rest[2 * PAGES_PER_STEP:]
    step = pl.program_id(1)
    ql = ql_ref[...]
    qp = qp_ref[...]
    n_rows = ql.shape[0]

    @pl.when(step == 0)
    def _():
        cn = cn_ref[...].astype(BF16)
        kn = kn_ref[...].astype(BF16)
        s = (_dot_nt(ql, cn) + _dot_nt(qp, kn)) * SM_SCALE
        tok = lax.broadcasted_iota(jnp.int32, s.shape, 0) // B_HEADS
        col = lax.broadcasted_iota(jnp.int32, s.shape, 1)
        s = jnp.where(col <= tok, s, NEG_INF)
        m = jnp.max(s, axis=-1, keepdims=True)
        p = jnp.exp(s - m)
        m_ref[...] = m
        l_ref[...] = jnp.sum(p, axis=-1, keepdims=True)
        acc_ref[...] = _dot(p.astype(BF16), cn)

    cb = jnp.concatenate([r[...].astype(BF16) for r in ck_refs], axis=0)
    kb = jnp.concatenate([r[...].astype(BF16) for r in kp_refs], axis=1)
    s = (_dot_nt(ql, cb) + _dot(qp, kb)) * SM_SCALE
    m = m_ref[...]
    m_new = jnp.maximum(m, jnp.max(s, axis=-1, keepdims=True))
    alpha = jnp.exp(m - m_new)
    p = jnp.exp(s - m_new)
    m_ref[...] = m_new
    l_ref[...] = alpha * l_ref[...] + jnp.sum(p, axis=-1, keepdims=True)
    acc_ref[...] = alpha * acc_ref[...] + _dot(p.astype(BF16), cb)

    @pl.when(step == pl.num_programs(1) - 1)
    def _():
        o = (acc_ref[...] / l_ref[...]).astype(BF16)
        full = _dot(o, wuv_ref[...])
        full = full.reshape(n_new, B_HEADS, B_HEADS * V_HD) * hm_ref[...]
        o_ref[...] = jnp.sum(full, axis=1)


def _mla_sample(q_lat, q_pe, c_new, k_new, cache_ckv, cache_kpe, page_table, wuv_all, head_mask,
                layer, n_new):
    n_seq, n_pages = page_table.shape
    n_steps = n_pages // PAGES_PER_STEP
    n_rows = q_lat.shape[1]

    def page_spec(shape, k):
        return pl.BlockSpec((None, None) + shape,
                            lambda b, s, pt: (layer, pt[b, s * PAGES_PER_STEP + k], 0, 0))

    in_specs = [pl.BlockSpec((None, n_rows, KV_LORA), lambda b, s, pt: (b, 0, 0)),
                pl.BlockSpec((None, n_rows, QK_ROPE), lambda b, s, pt: (b, 0, 0)),
                pl.BlockSpec((None, SUBLANES, KV_LORA), lambda b, s, pt: (b, 0, 0)),
                pl.BlockSpec((None, SUBLANES, QK_ROPE), lambda b, s, pt: (b, 0, 0))]
    in_specs += [page_spec((PAGE_SIZE, KV_LORA), k) for k in range(PAGES_PER_STEP)]
    in_specs += [page_spec((QK_ROPE, PAGE_SIZE), k) for k in range(PAGES_PER_STEP)]
    in_specs += [pl.BlockSpec(wuv_all.shape, lambda b, s, pt: (0, 0)),
                 pl.BlockSpec(head_mask.shape, lambda b, s, pt: (0, 0))]
    grid_spec = pltpu.PrefetchScalarGridSpec(
        num_scalar_prefetch=1,
        grid=(n_seq, n_steps),
        in_specs=in_specs,
        out_specs=pl.BlockSpec((None, n_new, B_HEADS * V_HD), lambda b, s, pt: (b, 0, 0)),
        scratch_shapes=[pltpu.VMEM((n_rows, 1), F32), pltpu.VMEM((n_rows, 1), F32),
                        pltpu.VMEM((n_rows, KV_LORA), F32)],
    )
    return pl.pallas_call(
        functools.partial(_mla_s_kernel, n_new),
        grid_spec=grid_spec,
        out_shape=jax.ShapeDtypeStruct((n_seq, n_new, B_HEADS * V_HD), F32),
        compiler_params=_params("parallel", "arbitrary"),
        name="mla_sample",
    )(page_table, q_lat, q_pe, c_new, k_new, *([cache_ckv] * PAGES_PER_STEP),
      *([jnp.swapaxes(cache_kpe, 2, 3)] * PAGES_PER_STEP), wuv_all, head_mask)


def _kc_kernel(x_ref, w_ref, gb_ref, q_out, k_out, v_out, o_out, g_out):
    proj = _dot(x_ref[...].astype(BF16), w_ref[...])
    q_out[...] = (proj[:, :OFF_MK] * (C_DK ** -0.5)).astype(BF16)
    k_out[...] = proj[:, OFF_MK:OFF_MV].astype(BF16)
    v_out[...] = proj[:, OFF_MV:OFF_MO].astype(BF16)
    o_out[...] = jax.nn.sigmoid(proj[:, OFF_MO:OFF_MG])
    pre = proj[:, OFF_MG:] + gb_ref[...]
    cap = GATE_CAP * jnp.tanh(pre / GATE_CAP)
    lane = lax.broadcasted_iota(jnp.int32, cap.shape, 1)
    g_out[...] = jnp.where(lane < C_HEADS, cap, jax.nn.log_sigmoid(cap))


def _layer1_prep(x, w_in, gate_bias, tm):
    rows = x.shape[0]
    widths = [OFF_MK, OFF_MK, MIX_O, MIX_O, LANES]
    dtypes = [BF16, BF16, BF16, F32, F32]
    return pl.pallas_call(
        _kc_kernel,
        grid=(rows // tm,),
        in_specs=[pl.BlockSpec((tm, D_MODEL), lambda i: (i, 0)), _full(w_in.shape),
                  _full(gate_bias.shape)],
        out_specs=[pl.BlockSpec((tm, wd), lambda i: (i, 0)) for wd in widths],
        out_shape=[jax.ShapeDtypeStruct((rows, wd), dt) for wd, dt in zip(widths, dtypes)],
        compiler_params=_params("parallel"),
        name="layer1_prep",
    )(x, w_in, gate_bias)


def _mlstm_kernel(chunk, q_ref, k_ref, v_ref, g_ref, c0_ref, n0_ref, m0_ref, h_ref, co_ref,
                  no_ref, mo_ref, c_s, n_s, m_s):
    ci = pl.program_id(1)

    @pl.when(ci == 0)
    def _():
        c_s[...] = c0_ref[...]
        n_s[...] = n0_ref[...]
        m_s[...] = m0_ref[...]

    g = g_ref[...]
    row = lax.broadcasted_iota(jnp.int32, g.shape, 0)
    lane = lax.broadcasted_iota(jnp.int32, g.shape, 1)
    b = g
    shift = 1
    while shift < chunk:
        b = b + jnp.where(row >= shift, pltpu.roll(b, shift, 0), 0.0)
        shift *= 2
    g_t = g.T
    b_t = b.T
    rr = lax.broadcasted_iota(jnp.int32, (chunk, chunk), 0)
    cc = lax.broadcasted_iota(jnp.int32, (chunk, chunk), 1)
    causal = cc <= rr
    for h in range(C_HEADS):
        b_col = jnp.sum(jnp.where(lane == C_HEADS + h, b, 0.0), axis=-1, keepdims=True)
        i_col = jnp.sum(jnp.where(lane == h, g, 0.0), axis=-1, keepdims=True)
        b_row = b_t[C_HEADS + h:C_HEADS + h + 1, :]
        i_row = g_t[h:h + 1, :]
        m_prev = m_s[h:h + 1, 0:1]
        d = jnp.where(causal, b_col - b_row + i_row, NEG_INF)
        gg = b_col + m_prev
        mt = jnp.maximum(gg, jnp.max(d, axis=-1, keepdims=True))
        w_inter = jnp.exp(gg - mt)
        qh = q_ref[:, h * HEAD_PAD:(h + 1) * HEAD_PAD]
        kh = k_ref[:, h * HEAD_PAD:(h + 1) * HEAD_PAD]
        vh = v_ref[:, h * C_DV:(h + 1) * C_DV]
        c_prev = c_s[h]
        n_prev = n_s[h:h + 1, :]
        att = jnp.exp(d - mt) * _dot_nt(qh, kh)
        num = w_inter * _dot_nt(qh, c_prev.astype(BF16)) + _dot(att.astype(BF16), vh)
        qn = jnp.sum(qh.astype(F32) * n_prev, axis=-1, keepdims=True)
        den = w_inter * qn + jnp.sum(att, axis=-1, keepdims=True)
        h_ref[:, h * C_DV:(h + 1) * C_DV] = num / jnp.maximum(jnp.abs(den), jnp.exp(-mt))
        m_new = mt[chunk - 1:chunk, :]
        b_last = b_col[chunk - 1:chunk, :]
        w_state = jnp.exp(b_last - b_col + i_col - m_new)
        dec = jnp.exp(b_last + m_prev - m_new)
        wv = (w_state * vh.astype(F32)).astype(BF16)
        c_s[h] = dec * c_prev + _dot_tn(wv, kh)
        n_s[h:h + 1, :] = dec * n_prev + jnp.sum(w_state * kh.astype(F32), axis=0, keepdims=True)
        m_s[h:h + 1, :] = jnp.broadcast_to(m_new, (1, LANES))

    @pl.when(ci == pl.num_programs(1) - 1)
    def _():
        co_ref[...] = c_s[...]
        no_ref[...] = n_s[...]
        mo_ref[...] = m_s[...]


def _mlstm(q, k, v, gates, c0, n0, m0, bsz, t_len, chunk):
    nc = t_len // chunk
    seq = lambda wd: pl.BlockSpec((chunk, wd), lambda b, c: (b * nc + c, 0))
    c_spec = pl.BlockSpec((None, C_HEADS, C_DV, HEAD_PAD), lambda b, c: (b, 0, 0, 0))
    v_spec = pl.BlockSpec((None, C_HEADS, LANES), lambda b, c: (b, 0, 0))
    return pl.pallas_call(
        functools.partial(_mlstm_kernel, chunk),
        grid=(bsz, nc),
        in_specs=[seq(OFF_MK), seq(OFF_MK), seq(MIX_O), seq(LANES), c_spec, v_spec, v_spec],
        out_specs=[seq(MIX_O), c_spec, v_spec, v_spec],
        out_shape=[jax.ShapeDtypeStruct((bsz * t_len, MIX_O), F32),
                   jax.ShapeDtypeStruct(c0.shape, F32),
                   jax.ShapeDtypeStruct(n0.shape, F32),
                   jax.ShapeDtypeStruct(m0.shape, F32)],
        scratch_shapes=[pltpu.VMEM((C_HEADS, C_DV, HEAD_PAD), F32),
                        pltpu.VMEM((C_HEADS, LANES), F32), pltpu.VMEM((C_HEADS, LANES), F32)],
        compiler_params=_params("parallel", "arbitrary"),
        name="mlstm",
    )(q, k, v, gates, c0, n0, m0)


def _kd_kernel(h_ref, o_ref, x_ref, mh_ref, wo_ref, ng_ref, nb_ref, out_ref):
    parts = []
    for h in range(C_HEADS):
        hh = h_ref[:, h * C_DV:(h + 1) * C_DV]
        parts.append(hh * lax.rsqrt(jnp.mean(hh * hh, axis=-1, keepdims=True) + 1e-6))
    hn = jnp.concatenate(parts, axis=-1) * mh_ref[...]
    y = _dot((hn * o_ref[...]).astype(BF16), wo_ref[...])
    out_ref[...] = _layer_norm_res(x_ref[...], y, ng_ref[...], nb_ref[...])


def _layer1_out(h, o_sig, x, mh_norm, w_out, ln_g, ln_b, tm):
    rows = x.shape[0]
    full = pl.BlockSpec((tm, D_MODEL), lambda i: (i, 0))
    consts = [mh_norm, w_out, ln_g, ln_b]
    return pl.pallas_call(
        _kd_kernel,
        grid=(rows // tm,),
        in_specs=[full, full, full] + [_full(c.shape) for c in consts],
        out_specs=full,
        out_shape=jax.ShapeDtypeStruct((rows, D_MODEL), F32),
        compiler_params=_params("parallel"),
        name="layer1_out",
    )(h, o_sig, x, *consts)


def _top16(s_ref, val_ref, idx_ref):
    n = s_ref.shape[0]
    row = lax.broadcasted_iota(jnp.int32, s_ref.shape, 0)
    s = s_ref[...]
    for r in range(PEER_TOPK):
        m = jnp.max(s, axis=0, keepdims=True)
        am = jnp.min(jnp.where(s == m, row, n), axis=0, keepdims=True)
        val_ref[r:r + 1, :] = m
        idx_ref[r:r + 1, :] = am
        s = jnp.where(row == am, NEG_INF, s)


def _pick_rank(src_ref, rank):
    out = jnp.zeros(rank.shape, jnp.int32)
    for r in range(PEER_TOPK):
        out = jnp.where(rank == r, src_ref[r:r + 1, :], out)
    return out


CAND_COUNT = [PEER_TOPK // (i + 1) for i in range(PEER_TOPK)]
CAND_START = [sum(CAND_COUNT[:i]) for i in range(PEER_TOPK + 1)]
CAND_ROWS = -(-CAND_START[PEER_TOPK] // SUBLANES) * SUBLANES


def _peer_q_kernel(tb, x_ref, wq_ref, k1_ref, k2_ref, i1_out, i2_out, g_out,
                   s_ref, cand_ref, t1_ref, j1_ref, t2_ref, j2_ref, tc_ref, jc_ref,
                   e1_ref, e2_ref, gt_ref):
    q = _dot(x_ref[...].astype(BF16), wq_ref[...])
    for h in range(PEER_HEADS):
        q1 = q[:, h * PEER_QDIM:h * PEER_QDIM + PEER_HALF].astype(BF16)
        q2 = q[:, h * PEER_QDIM + PEER_HALF:(h + 1) * PEER_QDIM].astype(BF16)
        s_ref[...] = _dot_nt(k1_ref[...], q1)
        _top16(s_ref, t1_ref, j1_ref)
        s_ref[...] = _dot_nt(k2_ref[...], q2)
        _top16(s_ref, t2_ref, j2_ref)
        for i in range(PEER_TOPK):
            lo, n = CAND_START[i], CAND_COUNT[i]
            cand_ref[lo:lo + n, :] = t1_ref[i:i + 1, :] + t2_ref[0:n, :]
        cand_ref[CAND_START[PEER_TOPK]:, :] = jnp.full(
            (CAND_ROWS - CAND_START[PEER_TOPK], tb), NEG_INF, F32)
        _top16(cand_ref, tc_ref, jc_ref)
        pos = jc_ref[...]
        rank1 = jnp.zeros_like(pos)
        start = jnp.zeros_like(pos)
        for i in range(1, PEER_TOPK):
            past = pos >= CAND_START[i]
            rank1 = rank1 + past.astype(jnp.int32)
            start = jnp.where(past, CAND_START[i], start)
        rows = slice(h * PEER_TOPK, (h + 1) * PEER_TOPK)
        e1_ref[rows, :] = _pick_rank(j1_ref, rank1).astype(F32)
        e2_ref[rows, :] = _pick_rank(j2_ref, pos - start).astype(F32)
        top = tc_ref[...]
        e = jnp.exp(top - top[0:1, :])
        gt_ref[rows, :] = e / jnp.sum(e, axis=0, keepdims=True)
    i1_out[...] = e1_ref[...].T.astype(jnp.int32)
    i2_out[...] = e2_ref[...].T.astype(jnp.int32)
    g_out[...] = gt_ref[...].T


def _peer_select(x, wq, k1, k2, tb):
    rows = x.shape[0]
    out = pl.BlockSpec((tb, PEER_PAIRS), lambda i: (i, 0))
    small = lambda dt: pltpu.VMEM((PEER_TOPK, tb), dt)
    return pl.pallas_call(
        functools.partial(_peer_q_kernel, tb),
        grid=(rows // tb,),
        in_specs=[pl.BlockSpec((tb, D_MODEL), lambda i: (i, 0)), _full(wq.shape), _full(k1.shape),
                  _full(k2.shape)],
        out_specs=[out, out, out],
        out_shape=[jax.ShapeDtypeStruct((rows, PEER_PAIRS), jnp.int32),
                   jax.ShapeDtypeStruct((rows, PEER_PAIRS), jnp.int32),
                   jax.ShapeDtypeStruct((rows, PEER_PAIRS), F32)],
        scratch_shapes=[pltpu.VMEM((N_KEYS, tb), F32),
                        pltpu.VMEM((CAND_ROWS, tb), F32),
                        small(F32), small(jnp.int32), small(F32), small(jnp.int32),
                        small(F32), small(jnp.int32),
                        pltpu.VMEM((PEER_PAIRS, tb), F32), pltpu.VMEM((PEER_PAIRS, tb), F32),
                        pltpu.VMEM((PEER_PAIRS, tb), F32)],
        compiler_params=_params("parallel"),
        name="peer_select",
    )(x, wq, k1, k2)


PEER_SPLIT = 2
TOKEN_SLOTS = 16
GATE_AHEAD = 8


def _peer_u_kernel(tb, n_i1, x_ref, u_ref, i1_ref, i2_ref, g_ref, a_ref, hs_ref, act_ref,
                   gm_ref):
    part = pl.program_id(0)
    xb = x_ref[...].astype(BF16)
    for pp in range(n_i1 // 2):
        h2 = _dot_nt(xb, u_ref[pp * 2 * N_KEYS:(pp + 1) * 2 * N_KEYS, :])
        hs_ref[(2 * pp) * tb:(2 * pp + 1) * tb, :] = h2[:, :N_KEYS]
        hs_ref[(2 * pp + 1) * tb:(2 * pp + 2) * tb, :] = h2[:, N_KEYS:]
    sub1 = lax.broadcasted_iota(jnp.int32, (n_i1, PEER_PAIRS), 0) + part * n_i1
    sub2 = lax.broadcasted_iota(jnp.int32, (N_KEYS, PEER_PAIRS), 0)

    def gate_mat(t, slot):
        i1 = i1_ref[pl.ds(t, 1), :]
        i2 = i2_ref[pl.ds(t, 1), :]
        gate = g_ref[pl.ds(t, 1), :]
        sel1 = jnp.where(sub1 == i1, gate, 0.0).astype(BF16)
        sel2 = jnp.where(sub2 == i2, 1.0, 0.0).astype(BF16)
        gm_ref[slot] = _dot_nt(sel1, sel2)

    def activate(t, slot):
        rows = pl.ds(t, n_i1, stride=tb)
        ht = hs_ref[rows, :]
        act = 0.5 * ht * (1.0 + lax.erf(ht * (2.0 ** -0.5)))
        act_ref[rows, :] = act * gm_ref[slot]

    for k in range(GATE_AHEAD):
        gate_mat(k, k)

    def slots(it, carry):
        for k in range(TOKEN_SLOTS):
            ahead = jnp.minimum(it * TOKEN_SLOTS + k + GATE_AHEAD, tb - 1)
            gate_mat(ahead, (k + GATE_AHEAD) % TOKEN_SLOTS)
            activate(it * TOKEN_SLOTS + k, k)
        return carry

    lax.fori_loop(0, tb // TOKEN_SLOTS, slots, 0)
    for i in range(n_i1):
        a_ref[:, i * N_KEYS:(i + 1) * N_KEYS] = act_ref[i * tb:(i + 1) * tb, :].astype(BF16)


def _peer_act(x, u, i1, i2, gate, tb):
    rows = x.shape[0]
    n_i1 = N_KEYS // PEER_SPLIT
    n_exp = N_EXPERTS // PEER_SPLIT
    tok = lambda wd: pl.BlockSpec((tb, wd), lambda p, i: (i, 0))
    slab = pltpu.VMEM((n_i1 * tb, N_KEYS), F32)
    return pl.pallas_call(
        functools.partial(_peer_u_kernel, tb, n_i1),
        grid=(PEER_SPLIT, rows // tb),
        in_specs=[tok(D_MODEL),
                  pl.BlockSpec((n_exp, D_MODEL), lambda p, i: (p, 0),
                               pipeline_mode=pl.Buffered(1)),
                  tok(PEER_PAIRS), tok(PEER_PAIRS), tok(PEER_PAIRS)],
        out_specs=pl.BlockSpec((tb, n_exp), lambda p, i: (i, p)),
        out_shape=jax.ShapeDtypeStruct((rows, N_EXPERTS), BF16),
        scratch_shapes=[slab, slab, pltpu.VMEM((TOKEN_SLOTS, n_i1, N_KEYS), F32)],
        compiler_params=_params("arbitrary", "arbitrary"),
        name="peer_act",
    )(x, u, i1, i2, gate)


def _peer_v_kernel(a_ref, v_ref, x_ref, ng_ref, nb_ref, o_ref, acc_ref):
    kk = pl.program_id(1)

    @pl.when(kk == 0)
    def _():
        acc_ref[...] = jnp.zeros_like(acc_ref)

    acc_ref[...] += _dot(a_ref[...], v_ref[...])

    @pl.when(kk == pl.num_programs(1) - 1)
    def _():
        o_ref[...] = _layer_norm_res(x_ref[...], acc_ref[...], ng_ref[...], nb_ref[...])


def _peer_out(a, v, x, ln_g, ln_b, tm, tk):
    rows = x.shape[0]
    return pl.pallas_call(
        _peer_v_kernel,
        grid=(rows // tm, N_EXPERTS // tk),
        in_specs=[pl.BlockSpec((tm, tk), lambda i, k: (i, k)),
                  pl.BlockSpec((tk, D_MODEL), lambda i, k: (k, 0)),
                  pl.BlockSpec((tm, D_MODEL), lambda i, k: (i, 0)),
                  pl.BlockSpec(ln_g.shape, lambda i, k: (0, 0)),
                  pl.BlockSpec(ln_b.shape, lambda i, k: (0, 0))],
        out_specs=pl.BlockSpec((tm, D_MODEL), lambda i, k: (i, 0)),
        out_shape=jax.ShapeDtypeStruct((rows, D_MODEL), F32),
        scratch_shapes=[pltpu.VMEM((tm, D_MODEL), F32)],
        compiler_params=_params("parallel", "arbitrary"),
        name="peer_out",
    )(a, v, x, ln_g, ln_b)


PEER_TB_SELECT = 256
PEER_TB_ACT = 256
PEER_TM_OUT, PEER_TK_OUT = 1024, 2048


def _peer(x, pw, ln_g, ln_b):
    rows = x.shape[0]
    i1, i2, gate = _peer_select(x, pw["wq"], pw["k1"], pw["k2"], min(PEER_TB_SELECT, rows))
    act = _peer_act(x, pw["u"], i1, i2, gate, min(PEER_TB_ACT, rows))
    return _peer_out(act, pw["v"], x, ln_g, ln_b, min(PEER_TM_OUT, rows), PEER_TK_OUT)


def _pad_cols(w, width):
    return jnp.pad(w, ((0, 0), (0, width - w.shape[1])))


def _row(v):
    return v.reshape(1, -1).astype(F32)


def _layer0_weights(w_in, mu, w0, w2, a0, a2, g2, kk, ka, rk, lnx_w, lnx_b, q_norm, w_uq, kv_norm,
                    w_uk, w_uv, w_out):
    c0 = 3 * A_W
    kpe0 = RWKV_COLS + Q_LORA + KV_LORA
    tile = lambda cols: jnp.tile(cols, (1, B_HEADS))
    w_in_p = jnp.concatenate([
        w_in[:, :c0],
        _pad_cols(w_in[:, c0:c0 + LORA_W], LANES),
        _pad_cols(w_in[:, c0 + LORA_W:c0 + LORA_W + LORA_A], LANES),
        w_in[:, c0 + LORA_W + LORA_A:RWKV_COLS],
        w_in[:, RWKV_COLS:kpe0],
        tile(w_in[:, kpe0:kpe0 + ROPE_HALF]),
        tile(w_in[:, kpe0 + ROPE_HALF:kpe0 + QK_ROPE]),
    ], axis=1).astype(BF16)
    mu_p = jnp.concatenate([mu[:c0], jnp.pad(mu[c0:c0 + LORA_W], (0, LANES - LORA_W)),
                            jnp.pad(mu[c0 + LORA_W:c0 + LORA_W + LORA_A], (0, LANES - LORA_A)),
                            mu[c0 + LORA_W + LORA_A:]])
    wq3 = w_uq.reshape(Q_LORA, B_HEADS, QK_NOPE + QK_ROPE)
    nope = jnp.pad(wq3[:, :, :QK_NOPE], ((0, 0), (0, 0), (0, LANES - QK_NOPE)))
    w_uq_p = jnp.concatenate([
        nope.reshape(Q_LORA, B_HEADS * LANES),
        wq3[:, :, QK_NOPE:QK_NOPE + ROPE_HALF].reshape(Q_LORA, LANES),
        wq3[:, :, QK_NOPE + ROPE_HALF:].reshape(Q_LORA, LANES),
    ], axis=1).astype(BF16)
    w_uk_t = jnp.pad(jnp.transpose(w_uk, (1, 2, 0)), ((0, 0), (0, LANES - QK_NOPE), (0, 0)))
    heads = np.arange(B_HEADS)
    blk = np.zeros((B_HEADS, KV_LORA, B_HEADS * V_HD), np.float32)
    hm = np.zeros((B_HEADS, B_HEADS * V_HD), np.float32)
    pem = np.zeros((B_HEADS, 2 * LANES), np.float32)
    for h in heads:
        blk[h, :, h * V_HD:(h + 1) * V_HD] = 1.0
        hm[h, h * V_HD:(h + 1) * V_HD] = 1.0
        pem[h, h * ROPE_HALF:(h + 1) * ROPE_HALF] = 1.0
        pem[h, LANES + h * ROPE_HALF:LANES + (h + 1) * ROPE_HALF] = 1.0
    wuv_all = w_uv.reshape(KV_LORA, B_HEADS * V_HD)
    seg = np.kron(np.eye(A_HEADS, dtype=np.float32), np.ones((A_HD, A_HD), np.float32))
    return {
        "w_in": w_in_p, "mu": _row(mu_p), "w0": _row(w0),
        "w2": jnp.pad(w2, ((0, LANES - LORA_W), (0, 0))).astype(BF16), "a0": _row(a0),
        "a2": jnp.pad(a2, ((0, LANES - LORA_A), (0, 0))).astype(BF16), "g2": g2.astype(BF16),
        "kk": _row(kk), "ka": _row(ka), "rk": _row(rk), "lnx_w": _row(lnx_w), "lnx_b": _row(lnx_b),
        "q_norm": _row(q_norm), "w_uq": w_uq_p, "kv_norm": _row(kv_norm),
        "w_uk": w_uk_t.astype(BF16),
        "wuv_pad": (wuv_all[None] * jnp.asarray(blk)).astype(BF16),
        "wuv_all": wuv_all.astype(BF16), "head_mask": jnp.asarray(hm),
        "pe_mask": jnp.asarray(pem), "seg": jnp.asarray(seg, dtype=BF16),
        "w_out": w_out.astype(BF16),
    }


def _layer1_weights(w_in, b_i, b_f, mh_norm, w_out):
    hk = C_HEADS * C_DK
    padh = lambda w: jnp.pad(w.reshape(D_MODEL, C_HEADS, C_DK),
                             ((0, 0), (0, 0), (0, HEAD_PAD - C_DK))).reshape(D_MODEL, OFF_MK)
    w_in_p = jnp.concatenate([
        padh(w_in[:, :hk]), padh(w_in[:, hk:2 * hk]), w_in[:, 2 * hk:2 * hk + 2 * MIX_O],
        _pad_cols(w_in[:, 2 * hk + 2 * MIX_O:], LANES),
    ], axis=1).astype(BF16)
    gate_bias = jnp.pad(jnp.concatenate([b_i, b_f]), (0, LANES - 2 * C_HEADS)).reshape(1, LANES)
    return {"w_in": w_in_p, "gate_bias": gate_bias.astype(F32), "mh_norm": _row(mh_norm),
            "w_out": w_out.astype(BF16)}


def _rope_tables(pos):
    inv = ROPE_BASE ** (-jnp.arange(0, QK_ROPE, 2, dtype=F32) / QK_ROPE)
    ang = pos.astype(F32)[:, None] * inv[None, :]
    return jnp.tile(jnp.cos(ang), (1, B_HEADS)), jnp.tile(jnp.sin(ang), (1, B_HEADS))


def _unpack_kpe(kpe_rot, bsz, t_len):
    return jnp.concatenate([kpe_rot[:, :ROPE_HALF], kpe_rot[:, LANES:LANES + ROPE_HALF]],
                           axis=-1).reshape(bsz, t_len, QK_ROPE)


def kernel(x_prompt, x_sample, cache_ckv, cache_kpe, page_table, state_shift, state_wkv, state_mlstm_c, state_mlstm_n, state_mlstm_m, w_in_e, mu_e, w0_e, w2_e, a0_e, a2_e, g2_e, kk_e, ka_e, rk_e, lnx_w_e, lnx_b_e, q_norm_e, w_uq_e, kv_norm_e, w_uk_e, w_uv_e, w_out_e, w_in_o, b_i_o, b_f_o, mh_norm_o, w_out_o, peer_wq, peer_k1, peer_k2, peer_u, peer_v, ln1_g, ln1_b, ln2_g, ln2_b):
    bp, tp, _ = x_prompt.shape
    bs, ts, _ = x_sample.shape
    past_len = page_table.shape[1] * PAGE_SIZE
    xp = x_prompt.reshape(bp * tp, D_MODEL)
    xs = x_sample.reshape(bs * ts, D_MODEL)
    rows_s = bs * ts
    tm_s = min(256, rows_s)
    tm_p = 256

    e = 0
    w0 = _layer0_weights(w_in_e[e], mu_e[e], w0_e[e], w2_e[e], a0_e[e], a2_e[e], g2_e[e], kk_e[e],
                         ka_e[e], rk_e[e], lnx_w_e[e], lnx_b_e[e], q_norm_e[e], w_uq_e[e],
                         kv_norm_e[e], w_uk_e[e], w_uv_e[e], w_out_e[e])
    g1, b1 = _row(ln1_g[0]), _row(ln1_b[0])
    cos_p, sin_p = _rope_tables(jnp.arange(tp))
    cos_s, sin_s = _rope_tables(past_len + jnp.arange(ts))
    cos_s = jnp.tile(cos_s, (tm_s // ts, 1))
    sin_s = jnp.tile(sin_s, (tm_s // ts, 1))

    (r, w, k, v, kk, kka, g, bonus, qcat, _, c_p, kpe_p, kcat) = _layer0_prep(
        xp, None, w0, cos_p, sin_p, bp, tp // tm_p, tm_p, tp)
    ch = lambda a: _to_chains(a, bp, tp)
    s0_p = jnp.zeros((A_HD, A_HD, bp * A_HEADS), F32)
    y, s_p = _wkv_scan(ch(r), ch(w), ch(k), ch(v), ch(kk), ch(kka), s0_p, 32)
    y = _from_chains(y, bp, tp)
    out_b = _mla_prompt(qcat, kcat, w0["wuv_pad"], bp, tp, 256)
    xp1 = _layer0_out(y, g, bonus, out_b, xp, w0, g1, b1, tm_p)
    wkv_p = jnp.transpose(s_p.reshape(A_HD, A_HD, bp, A_HEADS), (2, 3, 1, 0))

    start = _mm(state_shift[e], w0["w_in"][:, :RW_PAD], bs)
    start = jnp.repeat(start, ts, axis=0)
    (r, w, k, v, kk, kka, g, bonus, qcat, qpe, c_s, kpe_s, _) = _layer0_prep(
        xs, start, w0, cos_s, sin_s, rows_s // tm_s, 1, tm_s, ts)
    ch = lambda a: _to_chains(a, bs, ts)
    s0_s = jnp.transpose(state_wkv[e], (3, 2, 0, 1)).reshape(A_HD, A_HD, bs * A_HEADS)
    y, s_s = _wkv_scan(ch(r), ch(w), ch(k), ch(v), ch(kk), ch(kka), s0_s, ts)
    y = _from_chains(y, bs, ts)
    q_lat = qcat.reshape(bs, ts * B_HEADS, QCAT)[:, :, :KV_LORA]
    q_pe = jnp.transpose(qpe.reshape(bs, ts, 2, B_HEADS, ROPE_HALF), (0, 1, 3, 2, 4))
    q_pe = q_pe.reshape(bs, ts * B_HEADS, QK_ROPE).astype(BF16)
    kpe_new = _unpack_kpe(kpe_s, bs, ts)
    pad_new = lambda a: jnp.pad(a, ((0, 0), (0, SUBLANES - ts), (0, 0)))
    out_b = _mla_sample(q_lat, q_pe, pad_new(c_s.reshape(bs, ts, KV_LORA)), pad_new(kpe_new),
                        cache_ckv, cache_kpe, page_table, w0["wuv_all"], w0["head_mask"], e, ts)
    xs1 = _layer0_out(y, g, bonus, out_b.reshape(rows_s, B_HEADS * V_HD), xs, w0, g1, b1, tm_s)
    wkv_s = jnp.transpose(s_s.reshape(A_HD, A_HD, bs, A_HEADS), (2, 3, 1, 0))

    new_ckv_prompt = c_p.reshape(1, bp, tp, KV_LORA)
    new_kpe_prompt = _unpack_kpe(kpe_p, bp, tp)[None]
    new_ckv_sample = c_s.reshape(1, bs, ts, KV_LORA)
    new_kpe_sample = kpe_new[None]
    new_shift_prompt = x_prompt[:, -1][None]
    new_shift_sample = x_sample[:, -1][None]

    def peer_weights(l):
        return {"wq": peer_wq[l].astype(BF16), "k1": peer_k1[l].astype(BF16),
                "k2": peer_k2[l].astype(BF16), "u": peer_u[l].astype(BF16),
                "v": peer_v[l].astype(BF16)}

    pw = peer_weights(0)
    g2n, b2n = _row(ln2_g[0]), _row(ln2_b[0])
    xp2 = _peer(xp1, pw, g2n, b2n)
    xs2 = _peer(xs1, pw, g2n, b2n)

    od = 0
    w1 = _layer1_weights(w_in_o[od], b_i_o[od], b_f_o[od], mh_norm_o[od], w_out_o[od])
    g1, b1 = _row(ln1_g[1]), _row(ln1_b[1])
    pad_dk = lambda a: jnp.pad(a, [(0, 0)] * (a.ndim - 1) + [(0, HEAD_PAD - C_DK)])

    q, k, v, o_sig, gates = _layer1_prep(xp2, w1["w_in"], w1["gate_bias"], tm_p)
    zc = jnp.zeros((bp, C_HEADS, C_DV, HEAD_PAD), F32)
    zn = jnp.zeros((bp, C_HEADS, LANES), F32)
    h, c_p1, n_p1, m_p1 = _mlstm(q, k, v, gates, zc, zn, zn, bp, tp, MLSTM_CHUNK)
    xp3 = _layer1_out(h, o_sig, xp2, w1["mh_norm"], w1["w_out"], g1, b1, tm_p)

    q, k, v, o_sig, gates = _layer1_prep(xs2, w1["w_in"], w1["gate_bias"], tm_s)
    lp = MLSTM_CHUNK
    pad_t = lambda a: jnp.pad(a.reshape(bs, ts, -1), ((0, 0), (0, lp - ts), (0, 0))).reshape(
        bs * lp, -1)
    lane = jnp.arange(LANES)
    gate_fill = jnp.where(lane < C_HEADS, GATE_PAD, 0.0).astype(F32)
    gates_p = jnp.concatenate(
        [gates.reshape(bs, ts, LANES), jnp.broadcast_to(gate_fill, (bs, lp - ts, LANES))],
        axis=1).reshape(bs * lp, LANES)
    m0 = jnp.broadcast_to(state_mlstm_m[od][:, :, None], (bs, C_HEADS, LANES))
    h, c_s1, n_s1, m_s1 = _mlstm(pad_t(q), pad_t(k), pad_t(v), gates_p, pad_dk(state_mlstm_c[od]),
                                 pad_dk(state_mlstm_n[od]), m0, bs, lp, lp)
    h = h.reshape(bs, lp, MIX_O)[:, :ts].reshape(rows_s, MIX_O)
    xs3 = _layer1_out(h, o_sig, xs2, w1["mh_norm"], w1["w_out"], g1, b1, tm_s)

    pw = peer_weights(1)
    g2n, b2n = _row(ln2_g[1]), _row(ln2_b[1])
    xp4 = _peer(xp3, pw, g2n, b2n)
    xs4 = _peer(xs3, pw, g2n, b2n)

    return (xp4.reshape(bp, tp, D_MODEL), xs4.reshape(bs, ts, D_MODEL),
            new_ckv_prompt, new_kpe_prompt, new_ckv_sample, new_kpe_sample,
            new_shift_prompt, new_shift_sample, wkv_p[None], wkv_s[None],
            c_p1[..., :C_DK][None], c_s1[..., :C_DK][None],
            n_p1[..., :C_DK][None], n_s1[..., :C_DK][None],
            m_p1[..., 0][None], m_s1[..., 0][None])
```

```python
import functools

import jax
import jax.numpy as jnp
import numpy as np
from jax import lax
from jax.experimental import pallas as pl
from jax.experimental.pallas import tpu as pltpu

F32 = jnp.float32
BF16 = jnp.bfloat16
NEG_INF = float("-inf")

LANES = 128
SUBLANES = 8
VMEM_LIMIT_BYTES = 56 * 1024 * 1024

D_MODEL = 1024
DEPTH = 2
PAGE_SIZE = 128
ALPHA = (2 * DEPTH) ** 0.25
LN_EPS = 1e-5
A_HEADS, A_HD = 8, 64
A_W = A_HEADS * A_HD
LORA_W, LORA_A, LORA_G = 64, 64, 128
RWKV_COLS = 3 * A_W + LORA_W + LORA_A + LORA_G
GN_EPS = 64e-5
B_HEADS, QK_NOPE, QK_ROPE, V_HD = 8, 64, 32, 64
Q_LORA, KV_LORA = 384, 256
SM_SCALE = (QK_NOPE + QK_ROPE) ** -0.5
ROPE_BASE = 10000.0
ROPE_HALF = QK_ROPE // 2
C_HEADS, C_DK, C_DV = 8, 64, 128
MIX_O = C_HEADS * C_DV
MLSTM_CHUNK = 64
GATE_CAP = 15.0
GATE_PAD = -1e30
N_KEYS = 128
N_EXPERTS = N_KEYS * N_KEYS
PEER_HEADS, PEER_QDIM, PEER_TOPK = 8, 256, 16
PEER_HALF = PEER_QDIM // 2
PEER_PAIRS = PEER_HEADS * PEER_TOPK

RW_PAD = 3 * A_W + 3 * LANES
OFF_CQ = RW_PAD
OFF_CKV = OFF_CQ + Q_LORA
OFF_KP1 = OFF_CKV + KV_LORA
OFF_KP2 = OFF_KP1 + LANES
IN_E_PAD = OFF_KP2 + LANES
Q_PAD = B_HEADS * LANES + 2 * LANES
QCAT = 2 * KV_LORA
HEAD_PAD = LANES
OFF_MK = C_HEADS * HEAD_PAD
OFF_MV = 2 * C_HEADS * HEAD_PAD
OFF_MO = OFF_MV + MIX_O
OFF_MG = OFF_MO + MIX_O
IN_O_PAD = OFF_MG + LANES


def _params(*sem):
    return pltpu.CompilerParams(dimension_semantics=sem, vmem_limit_bytes=VMEM_LIMIT_BYTES)


def _dot(a, b):
    return jnp.dot(a, b, preferred_element_type=F32)


def _dot_nt(a, b):
    return lax.dot_general(a, b, (((1,), (1,)), ((), ())), preferred_element_type=F32)


def _dot_tn(a, b):
    return lax.dot_general(a, b, (((0,), (0,)), ((), ())), preferred_element_type=F32)


def _seg_sum(x, seg):
    hi = x.astype(BF16)
    lo = (x - hi.astype(F32)).astype(BF16)
    return _dot(hi, seg) + _dot(lo, seg)


def _layer_norm_res(x, h, g, b):
    z = ALPHA * x + h
    mu = jnp.mean(z, axis=-1, keepdims=True)
    zc = z - mu
    var = jnp.mean(zc * zc, axis=-1, keepdims=True)
    return zc * lax.rsqrt(var + LN_EPS) * g + b


def _full(shape):
    nd = len(shape)
    return pl.BlockSpec(shape, lambda *_: (0,) * nd)


def _mm_kernel(x_ref, w_ref, o_ref):
    o_ref[...] = _dot(x_ref[...].astype(BF16), w_ref[...])


def _mm(x, w, tm):
    m, k = x.shape
    n = w.shape[1]
    return pl.pallas_call(
        _mm_kernel,
        grid=(m // tm,),
        in_specs=[pl.BlockSpec((tm, k), lambda i: (i, 0)), _full((k, n))],
        out_specs=pl.BlockSpec((tm, n), lambda i: (i, 0)),
        out_shape=jax.ShapeDtypeStruct((m, n), F32),
        compiler_params=_params("parallel"),
        name="mm",
    )(x, w)


def _ka_kernel(has_start, seq_len, tm, *refs):
    if has_start:
        x_ref, start_ref = refs[:2]
        refs = refs[2:]
    else:
        x_ref = refs[0]
        start_ref = None
        refs = refs[1:]
    (w_in_ref, mu_ref, w0_ref, w2_ref, a0_ref, a2_ref, g2_ref, kk_ref, ka_ref, rk_ref, qn_ref,
     wuq_ref, kvn_ref, wuk_ref, cos_ref, sin_ref, seg_ref, pem_ref,
     r_out, w_out, k_out, v_out, kk_out, kka_out, g_out, bonus_out, qcat_out, qpe_out, c_out,
     kpe_out, kcat_out, carry_ref) = refs
    t_blk = pl.program_id(1)
    proj = _dot(x_ref[...].astype(BF16), w_in_ref[...])
    p_rw = proj[:, :RW_PAD]
    row = lax.broadcasted_iota(jnp.int32, (tm, 1), 0)
    prev = pltpu.roll(p_rw, 1, 0)
    if has_start:
        prev = jnp.where(row % seq_len == 0, start_ref[...], prev)
    else:
        @pl.when(t_blk == 0)
        def _():
            carry_ref[...] = jnp.zeros_like(carry_ref)

        prev = jnp.where(row == 0, carry_ref[SUBLANES - 1:SUBLANES, :], prev)
        carry_ref[...] = p_rw[tm - SUBLANES:, :]
    p = p_rw + (prev - p_rw) * mu_ref[...]
    r = p[:, 0:A_W]
    k = p[:, A_W:2 * A_W]
    v = p[:, 2 * A_W:3 * A_W]
    lw = p[:, 3 * A_W:3 * A_W + LANES]
    la = p[:, 3 * A_W + LANES:3 * A_W + 2 * LANES]
    lg = p[:, 3 * A_W + 2 * LANES:RW_PAD]
    w_pre = w0_ref[...] + _dot(jnp.tanh(lw).astype(BF16), w2_ref[...])
    w_log = -jax.nn.softplus(-w_pre) - 0.5
    decay = jnp.exp(-jnp.exp(w_log))
    a = jax.nn.sigmoid(a0_ref[...] + _dot(la.astype(BF16), a2_ref[...]))
    g = _dot(jax.nn.sigmoid(lg).astype(BF16), g2_ref[...])
    seg = seg_ref[...]
    kk = k * kk_ref[...]
    kk = kk / jnp.maximum(jnp.sqrt(_seg_sum(kk * kk, seg)), 1e-12)
    k = k * (1.0 + (a - 1.0) * ka_ref[...])
    r_out[...] = r
    w_out[...] = decay
    k_out[...] = k
    v_out[...] = v
    kk_out[...] = kk
    kka_out[...] = kk * a
    g_out[...] = g
    bonus_out[...] = _seg_sum(r * k * rk_ref[...], seg) * v
    cq = proj[:, OFF_CQ:OFF_CKV]
    ckv = proj[:, OFF_CKV:OFF_KP1]
    kp1 = proj[:, OFF_KP1:OFF_KP2]
    kp2 = proj[:, OFF_KP2:IN_E_PAD]
    cos = cos_ref[...]
    sin = sin_ref[...]
    cqn = cq * lax.rsqrt(jnp.mean(cq * cq, axis=-1, keepdims=True) + 1e-6) * qn_ref[...]
    q = _dot(cqn.astype(BF16), wuq_ref[...])
    x1 = q[:, B_HEADS * LANES:B_HEADS * LANES + LANES]
    x2 = q[:, B_HEADS * LANES + LANES:Q_PAD]
    qpe = jnp.concatenate([x1 * cos - x2 * sin, x1 * sin + x2 * cos], axis=-1)
    qpe_out[...] = qpe
    for h in range(B_HEADS):
        q_lat = _dot(q[:, h * LANES:(h + 1) * LANES].astype(BF16), wuk_ref[h])
        qcat_out[:, h * QCAT:h * QCAT + KV_LORA] = q_lat.astype(BF16)
        qcat_out[:, h * QCAT + KV_LORA:(h + 1) * QCAT] = (qpe * pem_ref[h:h + 1, :]).astype(BF16)
    c = ckv * lax.rsqrt(jnp.mean(ckv * ckv, axis=-1, keepdims=True) + 1e-6) * kvn_ref[...]
    kpe = jnp.concatenate([kp1 * cos - kp2 * sin, kp1 * sin + kp2 * cos], axis=-1)
    c_out[...] = c
    kpe_out[...] = kpe
    kcat_out[:, :KV_LORA] = c.astype(BF16)
    kcat_out[:, KV_LORA:] = kpe.astype(BF16)


def _layer0_prep(x, start, wts, cos, sin, n_seq_blocks, n_t_blocks, tm, seq_len):
    rows = x.shape[0]
    has_start = start is not None
    rmap = lambda b, t: (b * n_t_blocks + t, 0)
    tmap = lambda b, t: (t, 0)
    ins = [x] + ([start] if has_start else [])
    in_specs = [pl.BlockSpec((tm, D_MODEL), rmap)]
    if has_start:
        in_specs.append(pl.BlockSpec((tm, RW_PAD), rmap))
    w_names = ("w_in", "mu", "w0", "w2", "a0", "a2", "g2", "kk", "ka", "rk", "q_norm", "w_uq",
               "kv_norm", "w_uk")
    for nme in w_names:
        ins.append(wts[nme])
        in_specs.append(_full(wts[nme].shape))
    ins += [cos, sin, wts["seg"], wts["pe_mask"]]
    in_specs += [pl.BlockSpec((tm, LANES), tmap), pl.BlockSpec((tm, LANES), tmap),
                 _full(wts["seg"].shape), _full(wts["pe_mask"].shape)]
    widths = [A_W] * 8 + [B_HEADS * QCAT, 2 * LANES, KV_LORA, 2 * LANES, QCAT]
    dtypes = [F32] * 8 + [BF16, F32, F32, F32, BF16]
    out_shape = [jax.ShapeDtypeStruct((rows, wd), dt) for wd, dt in zip(widths, dtypes)]
    out_specs = [pl.BlockSpec((tm, wd), rmap) for wd in widths]
    return pl.pallas_call(
        functools.partial(_ka_kernel, has_start, seq_len, tm),
        grid=(n_seq_blocks, n_t_blocks),
        in_specs=in_specs,
        out_specs=out_specs,
        out_shape=out_shape,
        scratch_shapes=[pltpu.VMEM((SUBLANES, RW_PAD), F32)],
        compiler_params=_params("parallel", "arbitrary"),
        name="layer0_prep",
    )(*ins)


def _wkv_kernel(tch, r_ref, w_ref, k_ref, v_ref, kk_ref, kka_ref, s0_ref, y_ref, sout_ref,
                s_ref, sa_ref):
    tc = pl.program_id(1)

    @pl.when(tc == 0)
    def _():
        s_ref[...] = s0_ref[...]

    acc = jnp.zeros((A_HD, LANES), F32)
    for j in range(A_HD):
        acc = acc + s_ref[j] * kk_ref[0, pl.ds(j, 1), :]
    sa_ref[...] = -acc

    def step(t, carry):
        tn = jnp.minimum(t + 1, tch - 1)
        sa = sa_ref[...]
        v_t = v_ref[t]
        y = jnp.zeros((A_HD, LANES), F32)
        san = jnp.zeros((A_HD, LANES), F32)
        for j in range(A_HD):
            row = pl.ds(j, 1)
            sj = s_ref[j] * w_ref[t, row, :] + sa * kka_ref[t, row, :] + v_t * k_ref[t, row, :]
            s_ref[j] = sj
            y = y + sj * r_ref[t, row, :]
            san = san + sj * kk_ref[tn, row, :]
        y_ref[t] = y
        sa_ref[...] = -san
        return carry

    lax.fori_loop(0, tch, step, 0)

    @pl.when(tc == pl.num_programs(1) - 1)
    def _():
        sout_ref[...] = s_ref[...]


def _wkv_scan(r, w, k, v, kk, kka, s0, tch):
    t_len, _, n_chains = r.shape
    pad = (-n_chains) % LANES
    if pad:
        padc = lambda a: jnp.pad(a, ((0, 0), (0, 0), (0, pad)))
        y, s = _wkv_scan(*(padc(a) for a in (r, w, k, v, kk, kka, s0)), tch)
        return y[..., :n_chains], s[..., :n_chains]
    chains = n_chains
    seq = pl.BlockSpec((tch, A_HD, LANES), lambda c, t: (t, 0, c))
    st = pl.BlockSpec((A_HD, A_HD, LANES), lambda c, t: (0, 0, c))
    return pl.pallas_call(
        functools.partial(_wkv_kernel, tch),
        grid=(chains // LANES, t_len // tch),
        in_specs=[seq] * 6 + [st],
        out_specs=[seq, st],
        out_shape=[jax.ShapeDtypeStruct((t_len, A_HD, chains), F32),
                   jax.ShapeDtypeStruct((A_HD, A_HD, chains), F32)],
        scratch_shapes=[pltpu.VMEM((A_HD, A_HD, LANES), F32), pltpu.VMEM((A_HD, LANES), F32)],
        compiler_params=_params("parallel", "arbitrary"),
        name="wkv_scan",
    )(r, w, k, v, kk, kka, s0)


def _to_chains(x, bsz, t_len):
    x = x.reshape(bsz, t_len, A_HEADS, A_HD)
    return jnp.transpose(x, (1, 3, 0, 2)).reshape(t_len, A_HD, bsz * A_HEADS)


def _from_chains(y, bsz, t_len):
    y = y.reshape(t_len, A_HD, bsz, A_HEADS)
    return jnp.transpose(y, (2, 0, 3, 1)).reshape(bsz * t_len, A_W)


def _kb_kernel(y_ref, g_ref, bonus_ref, ob_ref, x_ref, lw_ref, lb_ref, seg_ref, wo_ref, ng_ref,
               nb_ref, o_ref):
    seg = seg_ref[...]
    y = y_ref[...]
    mu = _seg_sum(y, seg) * (1.0 / A_HD)
    yc = y - mu
    var = _seg_sum(yc * yc, seg) * (1.0 / A_HD)
    y = yc * lax.rsqrt(var + GN_EPS) * lw_ref[...] + lb_ref[...] + bonus_ref[...]
    out_a = (y * g_ref[...]).astype(BF16)
    h = _dot(out_a, wo_ref[:A_W, :]) + _dot(ob_ref[...].astype(BF16), wo_ref[A_W:, :])
    o_ref[...] = _layer_norm_res(x_ref[...], h, ng_ref[...], nb_ref[...])


def _layer0_out(y, g, bonus, out_b, x, wts, ln_g, ln_b, tm):
    rows = x.shape[0]
    half = pl.BlockSpec((tm, A_W), lambda i: (i, 0))
    full = pl.BlockSpec((tm, D_MODEL), lambda i: (i, 0))
    consts = [wts["lnx_w"], wts["lnx_b"], wts["seg"], wts["w_out"], ln_g, ln_b]
    return pl.pallas_call(
        _kb_kernel,
        grid=(rows // tm,),
        in_specs=[half, half, half, half, full] + [_full(c.shape) for c in consts],
        out_specs=full,
        out_shape=jax.ShapeDtypeStruct((rows, D_MODEL), F32),
        compiler_params=_params("parallel"),
        name="layer0_out",
    )(y, g, bonus, out_b, x, *consts)


def _mla_p_kernel(tq, q_ref, kc_ref, wuv_ref, o_ref, m_ref, l_ref, acc_ref):
    qi = pl.program_id(1)
    rr = lax.broadcasted_iota(jnp.int32, (tq, tq), 0)
    cc = lax.broadcasted_iota(jnp.int32, (tq, tq), 1)
    causal = cc <= rr
    m_ref[...] = jnp.full(m_ref.shape, NEG_INF, F32)
    l_ref[...] = jnp.zeros(l_ref.shape, F32)
    acc_ref[...] = jnp.zeros(acc_ref.shape, F32)

    def chunk(c, mask):
        kc = kc_ref[pl.ds(pl.multiple_of(c * tq, tq), tq), :]
        for h in range(B_HEADS):
            s = _dot_nt(q_ref[:, h * QCAT:(h + 1) * QCAT], kc) * SM_SCALE
            if mask:
                s = jnp.where(causal, s, NEG_INF)
            m = m_ref[h]
            m_new = jnp.maximum(m, jnp.max(s, axis=-1, keepdims=True))
            alpha = jnp.exp(m - m_new)
            p = jnp.exp(s - m_new)
            m_ref[h] = m_new
            l_ref[h] = alpha * l_ref[h] + jnp.sum(p, axis=-1, keepdims=True)
            acc_ref[h] = alpha * acc_ref[h] + _dot(p.astype(BF16), kc[:, :KV_LORA])

    def body(c, carry):
        chunk(c, False)
        return carry

    lax.fori_loop(0, qi, body, 0)
    chunk(qi, True)
    out = jnp.zeros((tq, B_HEADS * V_HD), F32)
    for h in range(B_HEADS):
        out = out + _dot((acc_ref[h] / l_ref[h]).astype(BF16), wuv_ref[h])
    o_ref[...] = out


def _mla_prompt(qcat, kcat, wuv_pad, bsz, t_len, tq):
    nq = t_len // tq
    return pl.pallas_call(
        functools.partial(_mla_p_kernel, tq),
        grid=(bsz, nq),
        in_specs=[pl.BlockSpec((tq, B_HEADS * QCAT), lambda b, i: (b * nq + i, 0)),
                  pl.BlockSpec((t_len, QCAT), lambda b, i: (b, 0)),
                  _full(wuv_pad.shape)],
        out_specs=pl.BlockSpec((tq, B_HEADS * V_HD), lambda b, i: (b * nq + i, 0)),
        out_shape=jax.ShapeDtypeStruct((bsz * t_len, B_HEADS * V_HD), F32),
        scratch_shapes=[pltpu.VMEM((B_HEADS, tq, 1), F32), pltpu.VMEM((B_HEADS, tq, 1), F32),
                        pltpu.VMEM((B_HEADS, tq, KV_LORA), F32)],
        compiler_params=_params("parallel", "arbitrary"),
        name="mla_prompt",
    )(qcat, kcat, wuv_pad)


PAGES_PER_STEP = 32


def _mla_s_kernel(n_new, pt_ref, ql_ref, qp_ref, cn_ref, kn_ref, *rest):
    ck_refs = rest[:PAGES_PER_STEP]
    kp_refs = rest[PAGES_PER_STEP:2 * PAGES_PER_STEP]
    wuv_ref, hm_ref, o_ref, m_ref, l_ref, acc_ref = rest[2 * PAGES_PER_STEP:]
    step = pl.program_id(1)
    ql = ql_ref[...]
    qp = qp_ref[...]
    n_rows = ql.shape[0]

    @pl.when(step == 0)
    def _():
        cn = cn_ref[...].astype(BF16)
        kn = kn_ref[...].astype(BF16)
        s = (_dot_nt(ql, cn) + _dot_nt(qp, kn)) * SM_SCALE
        tok = lax.broadcasted_iota(jnp.int32, s.shape, 0) // B_HEADS
        col = lax.broadcasted_iota(jnp.int32, s.shape, 1)
        s = jnp.where(col <= tok, s, NEG_INF)
        m = jnp.max(s, axis=-1, keepdims=True)
        p = jnp.exp(s - m)
        m_ref[...] = m
        l_ref[...] = jnp.sum(p, axis=-1, keepdims=True)
        acc_ref[...] = _dot(p.astype(BF16), cn)

    cb = jnp.concatenate([r[...].astype(BF16) for r in ck_refs], axis=0)
    kb = jnp.concatenate([r[...].astype(BF16) for r in kp_refs], axis=1)
    s = (_dot_nt(ql, cb) + _dot(qp, kb)) * SM_SCALE
    m = m_ref[...]
    m_new = jnp.maximum(m, jnp.max(s, axis=-1, keepdims=True))
    alpha = jnp.exp(m - m_new)
    p = jnp.exp(s - m_new)
    m_ref[...] = m_new
    l_ref[...] = alpha * l_ref[...] + jnp.sum(p, axis=-1, keepdims=True)
    acc_ref[...] = alpha * acc_ref[...] + _dot(p.astype(BF16), cb)

    @pl.when(step == pl.num_programs(1) - 1)
    def _():
        o = (acc_ref[...] / l_ref[...]).astype(BF16)
        full = _dot(o, wuv_ref[...])
        full = full.reshape(n_new, B_HEADS, B_HEADS * V_HD) * hm_ref[...]
        o_ref[...] = jnp.sum(full, axis=1)


def _mla_sample(q_lat, q_pe, c_new, k_new, cache_ckv, cache_kpe, page_table, wuv_all, head_mask,
                layer, n_new):
    n_seq, n_pages = page_table.shape
    n_steps = n_pages // PAGES_PER_STEP
    n_rows = q_lat.shape[1]

    def page_spec(shape, k):
        return pl.BlockSpec((None, None) + shape,
                            lambda b, s, pt: (layer, pt[b, s * PAGES_PER_STEP + k], 0, 0))

    in_specs = [pl.BlockSpec((None, n_rows, KV_LORA), lambda b, s, pt: (b, 0, 0)),
                pl.BlockSpec((None, n_rows, QK_ROPE), lambda b, s, pt: (b, 0, 0)),
                pl.BlockSpec((None, SUBLANES, KV_LORA), lambda b, s, pt: (b, 0, 0)),
                pl.BlockSpec((None, SUBLANES, QK_ROPE), lambda b, s, pt: (b, 0, 0))]
    in_specs += [page_spec((PAGE_SIZE, KV_LORA), k) for k in range(PAGES_PER_STEP)]
    in_specs += [page_spec((QK_ROPE, PAGE_SIZE), k) for k in range(PAGES_PER_STEP)]
    in_specs += [pl.BlockSpec(wuv_all.shape, lambda b, s, pt: (0, 0)),
                 pl.BlockSpec(head_mask.shape, lambda b, s, pt: (0, 0))]
    grid_spec = pltpu.PrefetchScalarGridSpec(
        num_scalar_prefetch=1,
        grid=(n_seq, n_steps),
        in_specs=in_specs,
        out_specs=pl.BlockSpec((None, n_new, B_HEADS * V_HD), lambda b, s, pt: (b, 0, 0)),
        scratch_shapes=[pltpu.VMEM((n_rows, 1), F32), pltpu.VMEM((n_rows, 1), F32),
                        pltpu.VMEM((n_rows, KV_LORA), F32)],
    )
    return pl.pallas_call(
        functools.partial(_mla_s_kernel, n_new),
        grid_spec=grid_spec,
        out_shape=jax.ShapeDtypeStruct((n_seq, n_new, B_HEADS * V_HD), F32),
        compiler_params=_params("parallel", "arbitrary"),
        name="mla_sample",
    )(page_table, q_lat, q_pe, c_new, k_new, *([cache_ckv] * PAGES_PER_STEP),
      *([jnp.swapaxes(cache_kpe, 2, 3)] * PAGES_PER_STEP), wuv_all, head_mask)


def _kc_kernel(x_ref, w_ref, gb_ref, q_out, k_out, v_out, o_out, g_out):
    proj = _dot(x_ref[...].astype(BF16), w_ref[...])
    q_out[...] = (proj[:, :OFF_MK] * (C_DK ** -0.5)).astype(BF16)
    k_out[...] = proj[:, OFF_MK:OFF_MV].astype(BF16)
    v_out[...] = proj[:, OFF_MV:OFF_MO].astype(BF16)
    o_out[...] = jax.nn.sigmoid(proj[:, OFF_MO:OFF_MG])
    pre = proj[:, OFF_MG:] + gb_ref[...]
    cap = GATE_CAP * jnp.tanh(pre / GATE_CAP)
    lane = lax.broadcasted_iota(jnp.int32, cap.shape, 1)
    g_out[...] = jnp.where(lane < C_HEADS, cap, jax.nn.log_sigmoid(cap))


def _layer1_prep(x, w_in, gate_bias, tm):
    rows = x.shape[0]
    widths = [OFF_MK, OFF_MK, MIX_O, MIX_O, LANES]
    dtypes = [BF16, BF16, BF16, F32, F32]
    return pl.pallas_call(
        _kc_kernel,
        grid=(rows // tm,),
        in_specs=[pl.BlockSpec((tm, D_MODEL), lambda i: (i, 0)), _full(w_in.shape),
                  _full(gate_bias.shape)],
        out_specs=[pl.BlockSpec((tm, wd), lambda i: (i, 0)) for wd in widths],
        out_shape=[jax.ShapeDtypeStruct((rows, wd), dt) for wd, dt in zip(widths, dtypes)],
        compiler_params=_params("parallel"),
        name="layer1_prep",
    )(x, w_in, gate_bias)


def _mlstm_kernel(chunk, q_ref, k_ref, v_ref, g_ref, c0_ref, n0_ref, m0_ref, h_ref, co_ref,
                  no_ref, mo_ref, c_s, n_s, m_s):
    ci = pl.program_id(1)

    @pl.when(ci == 0)
    def _():
        c_s[...] = c0_ref[...]
        n_s[...] = n0_ref[...]
        m_s[...] = m0_ref[...]

    g = g_ref[...]
    row = lax.broadcasted_iota(jnp.int32, g.shape, 0)
    lane = lax.broadcasted_iota(jnp.int32, g.shape, 1)
    b = g
    shift = 1
    while shift < chunk:
        b = b + jnp.where(row >= shift, pltpu.roll(b, shift, 0), 0.0)
        shift *= 2
    g_t = g.T
    b_t = b.T
    rr = lax.broadcasted_iota(jnp.int32, (chunk, chunk), 0)
    cc = lax.broadcasted_iota(jnp.int32, (chunk, chunk), 1)
    causal = cc <= rr
    for h in range(C_HEADS):
        b_col = jnp.sum(jnp.where(lane == C_HEADS + h, b, 0.0), axis=-1, keepdims=True)
        i_col = jnp.sum(jnp.where(lane == h, g, 0.0), axis=-1, keepdims=True)
        b_row = b_t[C_HEADS + h:C_HEADS + h + 1, :]
        i_row = g_t[h:h + 1, :]
        m_prev = m_s[h:h + 1, 0:1]
        d = jnp.where(causal, b_col - b_row + i_row, NEG_INF)
        gg = b_col + m_prev
        mt = jnp.maximum(gg, jnp.max(d, axis=-1, keepdims=True))
        w_inter = jnp.exp(gg - mt)
        qh = q_ref[:, h * HEAD_PAD:(h + 1) * HEAD_PAD]
        kh = k_ref[:, h * HEAD_PAD:(h + 1) * HEAD_PAD]
        vh = v_ref[:, h * C_DV:(h + 1) * C_DV]
        c_prev = c_s[h]
        n_prev = n_s[h:h + 1, :]
        att = jnp.exp(d - mt) * _dot_nt(qh, kh)
        num = w_inter * _dot_nt(qh, c_prev.astype(BF16)) + _dot(att.astype(BF16), vh)
        qn = jnp.sum(qh.astype(F32) * n_prev, axis=-1, keepdims=True)
        den = w_inter * qn + jnp.sum(att, axis=-1, keepdims=True)
        h_ref[:, h * C_DV:(h + 1) * C_DV] = num / jnp.maximum(jnp.abs(den), jnp.exp(-mt))
        m_new = mt[chunk - 1:chunk, :]
        b_last = b_col[chunk - 1:chunk, :]
        w_state = jnp.exp(b_last - b_col + i_col - m_new)
        dec = jnp.exp(b_last + m_prev - m_new)
        wv = (w_state * vh.astype(F32)).astype(BF16)
        c_s[h] = dec * c_prev + _dot_tn(wv, kh)
        n_s[h:h + 1, :] = dec * n_prev + jnp.sum(w_state * kh.astype(F32), axis=0, keepdims=True)
        m_s[h:h + 1, :] = jnp.broadcast_to(m_new, (1, LANES))

    @pl.when(ci == pl.num_programs(1) - 1)
    def _():
        co_ref[...] = c_s[...]
        no_ref[...] = n_s[...]
        mo_ref[...] = m_s[...]


def _mlstm(q, k, v, gates, c0, n0, m0, bsz, t_len, chunk):
    nc = t_len // chunk
    seq = lambda wd: pl.BlockSpec((chunk, wd), lambda b, c: (b * nc + c, 0))
    c_spec = pl.BlockSpec((None, C_HEADS, C_DV, HEAD_PAD), lambda b, c: (b, 0, 0, 0))
    v_spec = pl.BlockSpec((None, C_HEADS, LANES), lambda b, c: (b, 0, 0))
    return pl.pallas_call(
        functools.partial(_mlstm_kernel, chunk),
        grid=(bsz, nc),
        in_specs=[seq(OFF_MK), seq(OFF_MK), seq(MIX_O), seq(LANES), c_spec, v_spec, v_spec],
        out_specs=[seq(MIX_O), c_spec, v_spec, v_spec],
        out_shape=[jax.ShapeDtypeStruct((bsz * t_len, MIX_O), F32),
                   jax.ShapeDtypeStruct(c0.shape, F32),
                   jax.ShapeDtypeStruct(n0.shape, F32),
                   jax.ShapeDtypeStruct(m0.shape, F32)],
        scratch_shapes=[pltpu.VMEM((C_HEADS, C_DV, HEAD_PAD), F32),
                        pltpu.VMEM((C_HEADS, LANES), F32), pltpu.VMEM((C_HEADS, LANES), F32)],
        compiler_params=_params("parallel", "arbitrary"),
        name="mlstm",
    )(q, k, v, gates, c0, n0, m0)


def _kd_kernel(h_ref, o_ref, x_ref, mh_ref, wo_ref, ng_ref, nb_ref, out_ref):
    parts = []
    for h in range(C_HEADS):
        hh = h_ref[:, h * C_DV:(h + 1) * C_DV]
        parts.append(hh * lax.rsqrt(jnp.mean(hh * hh, axis=-1, keepdims=True) + 1e-6))
    hn = jnp.concatenate(parts, axis=-1) * mh_ref[...]
    y = _dot((hn * o_ref[...]).astype(BF16), wo_ref[...])
    out_ref[...] = _layer_norm_res(x_ref[...], y, ng_ref[...], nb_ref[...])


def _layer1_out(h, o_sig, x, mh_norm, w_out, ln_g, ln_b, tm):
    rows = x.shape[0]
    full = pl.BlockSpec((tm, D_MODEL), lambda i: (i, 0))
    consts = [mh_norm, w_out, ln_g, ln_b]
    return pl.pallas_call(
        _kd_kernel,
        grid=(rows // tm,),
        in_specs=[full, full, full] + [_full(c.shape) for c in consts],
        out_specs=full,
        out_shape=jax.ShapeDtypeStruct((rows, D_MODEL), F32),
        compiler_params=_params("parallel"),
        name="layer1_out",
    )(h, o_sig, x, *consts)


def _top16(s_ref, val_ref, idx_ref):
    n = s_ref.shape[0]
    row = lax.broadcasted_iota(jnp.int32, s_ref.shape, 0)
    s = s_ref[...]
    for r in range(PEER_TOPK):
        m = jnp.max(s, axis=0, keepdims=True)
        am = jnp.min(jnp.where(s == m, row, n), axis=0, keepdims=True)
        val_ref[r:r + 1, :] = m
        idx_ref[r:r + 1, :] = am
        s = jnp.where(row == am, NEG_INF, s)


def _sort16_network():
    n, pairs, p = PEER_TOPK, [], 1
    while p < n:
        k = p
        while k >= 1:
            for j in range(k % p, n - k, 2 * k):
                for i in range(min(k, n - j - k)):
                    if (i + j) // (2 * p) == (i + j + k) // (2 * p):
                        pairs.append((i + j, i + j + k))
            k //= 2
        p *= 2
    return pairs


SORT16 = _sort16_network()


def _top16_keys(s_ref, val_ref, idx_ref):
    n_slab = N_KEYS // SUBLANES
    row = lax.broadcasted_iota(jnp.int32, (SUBLANES, LANES), 0)
    for g in range(s_ref.shape[1] // LANES):
        cols = slice(g * LANES, (g + 1) * LANES)
        vals = [s_ref[v * SUBLANES:(v + 1) * SUBLANES, cols] for v in range(n_slab)]
        keys = [row + v * SUBLANES for v in range(n_slab)]
        for a, b in SORT16:
            keep = (vals[a] > vals[b]) | ((vals[a] == vals[b]) & (keys[a] < keys[b]))
            vals[a], vals[b] = (jnp.where(keep, vals[a], vals[b]),
                                jnp.where(keep, vals[b], vals[a]))
            keys[a], keys[b] = (jnp.where(keep, keys[a], keys[b]),
                                jnp.where(keep, keys[b], keys[a]))
        for r in range(PEER_TOPK):
            m = jnp.max(vals[0], axis=0, keepdims=True)
            am = jnp.min(jnp.where(vals[0] == m, keys[0], N_KEYS), axis=0, keepdims=True)
            val_ref[r:r + 1, cols] = m
            idx_ref[r:r + 1, cols] = am
            taken = keys[0] == am
            for v in range(PEER_TOPK - 1 - r):
                vals[v] = jnp.where(taken, vals[v + 1], vals[v])
                keys[v] = jnp.where(taken, keys[v + 1], keys[v])


def _pick_rank(src_ref, rank):
    out = jnp.zeros(rank.shape, jnp.int32)
    for r in range(PEER_TOPK):
        out = jnp.where(rank == r, src_ref[r:r + 1, :], out)
    return out


CAND_COUNT = [PEER_TOPK // (i + 1) for i in range(PEER_TOPK)]
CAND_START = [sum(CAND_COUNT[:i]) for i in range(PEER_TOPK + 1)]
CAND_ROWS = -(-CAND_START[PEER_TOPK] // SUBLANES) * SUBLANES


def _peer_q_kernel(tb, x_ref, wq_ref, k1_ref, k2_ref, i1_out, i2_out, g_out,
                   s_ref, cand_ref, t1_ref, j1_ref, t2_ref, j2_ref, tc_ref, jc_ref,
                   e1_ref, e2_ref, gt_ref):
    q = _dot(x_ref[...].astype(BF16), wq_ref[...])
    for h in range(PEER_HEADS):
        q1 = q[:, h * PEER_QDIM:h * PEER_QDIM + PEER_HALF].astype(BF16)
        q2 = q[:, h * PEER_QDIM + PEER_HALF:(h + 1) * PEER_QDIM].astype(BF16)
        s_ref[...] = _dot_nt(k1_ref[...], q1)
        _top16_keys(s_ref, t1_ref, j1_ref)
        s_ref[...] = _dot_nt(k2_ref[...], q2)
        _top16_keys(s_ref, t2_ref, j2_ref)
        for i in range(PEER_TOPK):
            lo, n = CAND_START[i], CAND_COUNT[i]
            cand_ref[lo:lo + n, :] = t1_ref[i:i + 1, :] + t2_ref[0:n, :]
        cand_ref[CAND_START[PEER_TOPK]:, :] = jnp.full(
            (CAND_ROWS - CAND_START[PEER_TOPK], tb), NEG_INF, F32)
        _top16(cand_ref, tc_ref, jc_ref)
        pos = jc_ref[...]
        rank1 = jnp.zeros_like(pos)
        start = jnp.zeros_like(pos)
        for i in range(1, PEER_TOPK):
            past = pos >= CAND_START[i]
            rank1 = rank1 + past.astype(jnp.int32)
            start = jnp.where(past, CAND_START[i], start)
        rows = slice(h * PEER_TOPK, (h + 1) * PEER_TOPK)
        e1_ref[rows, :] = _pick_rank(j1_ref, rank1).astype(F32)
        e2_ref[rows, :] = _pick_rank(j2_ref, pos - start).astype(F32)
        top = tc_ref[...]
        e = jnp.exp(top - top[0:1, :])
        gt_ref[rows, :] = e / jnp.sum(e, axis=0, keepdims=True)
    i1_out[...] = e1_ref[...].T.astype(jnp.int32)
    i2_out[...] = e2_ref[...].T.astype(jnp.int32)
    g_out[...] = gt_ref[...].T


def _peer_select(x, wq, k1, k2, tb):
    rows = x.shape[0]
    out = pl.BlockSpec((tb, PEER_PAIRS), lambda i: (i, 0))
    small = lambda dt: pltpu.VMEM((PEER_TOPK, tb), dt)
    return pl.pallas_call(
        functools.partial(_peer_q_kernel, tb),
        grid=(rows // tb,),
        in_specs=[pl.BlockSpec((tb, D_MODEL), lambda i: (i, 0)), _full(wq.shape), _full(k1.shape),
                  _full(k2.shape)],
        out_specs=[out, out, out],
        out_shape=[jax.ShapeDtypeStruct((rows, PEER_PAIRS), jnp.int32),
                   jax.ShapeDtypeStruct((rows, PEER_PAIRS), jnp.int32),
                   jax.ShapeDtypeStruct((rows, PEER_PAIRS), F32)],
        scratch_shapes=[pltpu.VMEM((N_KEYS, tb), F32),
                        pltpu.VMEM((CAND_ROWS, tb), F32),
                        small(F32), small(jnp.int32), small(F32), small(jnp.int32),
                        small(F32), small(jnp.int32),
                        pltpu.VMEM((PEER_PAIRS, tb), F32), pltpu.VMEM((PEER_PAIRS, tb), F32),
                        pltpu.VMEM((PEER_PAIRS, tb), F32)],
        compiler_params=_params("parallel"),
        name="peer_select",
    )(x, wq, k1, k2)


PEER_SPLIT = 2
PAIR_UNROLL = 4


def _peer_u_kernel(tb, n_i1, x_ref, u_ref, i1_ref, i2_ref, g_ref, a_ref, hs_ref, gs_ref, xb_ref):
    part = pl.program_id(0)
    n_pairs = n_i1 // 2
    tokens_per_pair = tb // n_pairs
    xb_ref[...] = x_ref[...].astype(BF16)
    sub1 = lax.broadcasted_iota(jnp.int32, (n_i1, PEER_PAIRS), 0) + part * n_i1
    sub2 = lax.broadcasted_iota(jnp.int32, (N_KEYS, PEER_PAIRS), 0)

    def slab_pair(pp):
        u2 = u_ref[pl.ds(pl.multiple_of(pp * 2 * N_KEYS, 2 * N_KEYS), 2 * N_KEYS), :]
        h2 = _dot_nt(xb_ref[...], u2)
        act = 0.5 * h2 * (1.0 + lax.erf(h2 * (2.0 ** -0.5)))
        base = pl.multiple_of(pp * 2 * tb, 2 * tb)
        hs_ref[pl.ds(base, tb), :] = act[:, :N_KEYS]
        hs_ref[pl.ds(base + tb, tb), :] = act[:, N_KEYS:]

    def gate_rows(t):
        i1 = i1_ref[pl.ds(t, 1), :]
        i2 = i2_ref[pl.ds(t, 1), :]
        gate = g_ref[pl.ds(t, 1), :]
        sel1 = jnp.where(sub1 == i1, gate, 0.0).astype(BF16)
        sel2 = jnp.where(sub2 == i2, 1.0, 0.0).astype(BF16)
        gs_ref[pl.ds(t, n_i1, stride=tb), :] = _dot_nt(sel1, sel2)

    def pair(pp, carry):
        slab_pair(pp)
        for k in range(tokens_per_pair):
            gate_rows(pp * tokens_per_pair + k)
        return carry

    lax.fori_loop(0, n_pairs, pair, 0, unroll=PAIR_UNROLL)
    for i in range(n_i1):
        rows = slice(i * tb, (i + 1) * tb)
        a_ref[:, i * N_KEYS:(i + 1) * N_KEYS] = (hs_ref[rows, :] * gs_ref[rows, :]).astype(BF16)


def _peer_act(x, u, i1, i2, gate, tb):
    rows = x.shape[0]
    n_i1 = N_KEYS // PEER_SPLIT
    n_exp = N_EXPERTS // PEER_SPLIT
    assert tb % (n_i1 // 2) == 0
    tok = lambda wd: pl.BlockSpec((tb, wd), lambda p, i: (i, 0))
    slab = pltpu.VMEM((n_i1 * tb, N_KEYS), F32)
    return pl.pallas_call(
        functools.partial(_peer_u_kernel, tb, n_i1),
        grid=(PEER_SPLIT, rows // tb),
        in_specs=[tok(D_MODEL),
                  pl.BlockSpec((n_exp, D_MODEL), lambda p, i: (p, 0),
                               pipeline_mode=pl.Buffered(1)),
                  tok(PEER_PAIRS), tok(PEER_PAIRS), tok(PEER_PAIRS)],
        out_specs=pl.BlockSpec((tb, n_exp), lambda p, i: (i, p)),
        out_shape=jax.ShapeDtypeStruct((rows, N_EXPERTS), BF16),
        scratch_shapes=[slab, slab, pltpu.VMEM((tb, D_MODEL), BF16)],
        compiler_params=_params("arbitrary", "arbitrary"),
        name="peer_act",
    )(x, u, i1, i2, gate)


def _peer_v_kernel(a_ref, v_ref, x_ref, ng_ref, nb_ref, o_ref, acc_ref):
    kk = pl.program_id(1)

    @pl.when(kk == 0)
    def _():
        acc_ref[...] = jnp.zeros_like(acc_ref)

    acc_ref[...] += _dot(a_ref[...], v_ref[...])

    @pl.when(kk == pl.num_programs(1) - 1)
    def _():
        o_ref[...] = _layer_norm_res(x_ref[...], acc_ref[...], ng_ref[...], nb_ref[...])


def _peer_out(a, v, x, ln_g, ln_b, tm, tk):
    rows = x.shape[0]
    return pl.pallas_call(
        _peer_v_kernel,
        grid=(rows // tm, N_EXPERTS // tk),
        in_specs=[pl.BlockSpec((tm, tk), lambda i, k: (i, k)),
                  pl.BlockSpec((tk, D_MODEL), lambda i, k: (k, 0)),
                  pl.BlockSpec((tm, D_MODEL), lambda i, k: (i, 0)),
                  pl.BlockSpec(ln_g.shape, lambda i, k: (0, 0)),
                  pl.BlockSpec(ln_b.shape, lambda i, k: (0, 0))],
        out_specs=pl.BlockSpec((tm, D_MODEL), lambda i, k: (i, 0)),
        out_shape=jax.ShapeDtypeStruct((rows, D_MODEL), F32),
        scratch_shapes=[pltpu.VMEM((tm, D_MODEL), F32)],
        compiler_params=_params("parallel", "arbitrary"),
        name="peer_out",
    )(a, v, x, ln_g, ln_b)


PEER_TB_SELECT = 256
PEER_TB_ACT = 256
PEER_TM_OUT, PEER_TK_OUT = 1024, 2048


def _peer(x, pw, ln_g, ln_b):
    rows = x.shape[0]
    i1, i2, gate = _peer_select(x, pw["wq"], pw["k1"], pw["k2"], min(PEER_TB_SELECT, rows))
    act = _peer_act(x, pw["u"], i1, i2, gate, min(PEER_TB_ACT, rows))
    return _peer_out(act, pw["v"], x, ln_g, ln_b, min(PEER_TM_OUT, rows), PEER_TK_OUT)


def _pad_cols(w, width):
    return jnp.pad(w, ((0, 0), (0, width - w.shape[1])))


def _row(v):
    return v.reshape(1, -1).astype(F32)


def _layer0_weights(w_in, mu, w0, w2, a0, a2, g2, kk, ka, rk, lnx_w, lnx_b, q_norm, w_uq, kv_norm,
                    w_uk, w_uv, w_out):
    c0 = 3 * A_W
    kpe0 = RWKV_COLS + Q_LORA + KV_LORA
    tile = lambda cols: jnp.tile(cols, (1, B_HEADS))
    w_in_p = jnp.concatenate([
        w_in[:, :c0],
        _pad_cols(w_in[:, c0:c0 + LORA_W], LANES),
        _pad_cols(w_in[:, c0 + LORA_W:c0 + LORA_W + LORA_A], LANES),
        w_in[:, c0 + LORA_W + LORA_A:RWKV_COLS],
        w_in[:, RWKV_COLS:kpe0],
        tile(w_in[:, kpe0:kpe0 + ROPE_HALF]),
        tile(w_in[:, kpe0 + ROPE_HALF:kpe0 + QK_ROPE]),
    ], axis=1).astype(BF16)
    mu_p = jnp.concatenate([mu[:c0], jnp.pad(mu[c0:c0 + LORA_W], (0, LANES - LORA_W)),
                            jnp.pad(mu[c0 + LORA_W:c0 + LORA_W + LORA_A], (0, LANES - LORA_A)),
                            mu[c0 + LORA_W + LORA_A:]])
    wq3 = w_uq.reshape(Q_LORA, B_HEADS, QK_NOPE + QK_ROPE)
    nope = jnp.pad(wq3[:, :, :QK_NOPE], ((0, 0), (0, 0), (0, LANES - QK_NOPE)))
    w_uq_p = jnp.concatenate([
        nope.reshape(Q_LORA, B_HEADS * LANES),
        wq3[:, :, QK_NOPE:QK_NOPE + ROPE_HALF].reshape(Q_LORA, LANES),
        wq3[:, :, QK_NOPE + ROPE_HALF:].reshape(Q_LORA, LANES),
    ], axis=1).astype(BF16)
    w_uk_t = jnp.pad(jnp.transpose(w_uk, (1, 2, 0)), ((0, 0), (0, LANES - QK_NOPE), (0, 0)))
    heads = np.arange(B_HEADS)
    blk = np.zeros((B_HEADS, KV_LORA, B_HEADS * V_HD), np.float32)
    hm = np.zeros((B_HEADS, B_HEADS * V_HD), np.float32)
    pem = np.zeros((B_HEADS, 2 * LANES), np.float32)
    for h in heads:
        blk[h, :, h * V_HD:(h + 1) * V_HD] = 1.0
        hm[h, h * V_HD:(h + 1) * V_HD] = 1.0
        pem[h, h * ROPE_HALF:(h + 1) * ROPE_HALF] = 1.0
        pem[h, LANES + h * ROPE_HALF:LANES + (h + 1) * ROPE_HALF] = 1.0
    wuv_all = w_uv.reshape(KV_LORA, B_HEADS * V_HD)
    seg = np.kron(np.eye(A_HEADS, dtype=np.float32), np.ones((A_HD, A_HD), np.float32))
    return {
        "w_in": w_in_p, "mu": _row(mu_p), "w0": _row(w0),
        "w2": jnp.pad(w2, ((0, LANES - LORA_W), (0, 0))).astype(BF16), "a0": _row(a0),
        "a2": jnp.pad(a2, ((0, LANES - LORA_A), (0, 0))).astype(BF16), "g2": g2.astype(BF16),
        "kk": _row(kk), "ka": _row(ka), "rk": _row(rk), "lnx_w": _row(lnx_w), "lnx_b": _row(lnx_b),
        "q_norm": _row(q_norm), "w_uq": w_uq_p, "kv_norm": _row(kv_norm),
        "w_uk": w_uk_t.astype(BF16),
        "wuv_pad": (wuv_all[None] * jnp.asarray(blk)).astype(BF16),
        "wuv_all": wuv_all.astype(BF16), "head_mask": jnp.asarray(hm),
        "pe_mask": jnp.asarray(pem), "seg": jnp.asarray(seg, dtype=BF16),
        "w_out": w_out.astype(BF16),
    }


def _layer1_weights(w_in, b_i, b_f, mh_norm, w_out):
    hk = C_HEADS * C_DK
    padh = lambda w: jnp.pad(w.reshape(D_MODEL, C_HEADS, C_DK),
                             ((0, 0), (0, 0), (0, HEAD_PAD - C_DK))).reshape(D_MODEL, OFF_MK)
    w_in_p = jnp.concatenate([
        padh(w_in[:, :hk]), padh(w_in[:, hk:2 * hk]), w_in[:, 2 * hk:2 * hk + 2 * MIX_O],
        _pad_cols(w_in[:, 2 * hk + 2 * MIX_O:], LANES),
    ], axis=1).astype(BF16)
    gate_bias = jnp.pad(jnp.concatenate([b_i, b_f]), (0, LANES - 2 * C_HEADS)).reshape(1, LANES)
    return {"w_in": w_in_p, "gate_bias": gate_bias.astype(F32), "mh_norm": _row(mh_norm),
            "w_out": w_out.astype(BF16)}


def _rope_tables(pos):
    inv = ROPE_BASE ** (-jnp.arange(0, QK_ROPE, 2, dtype=F32) / QK_ROPE)
    ang = pos.astype(F32)[:, None] * inv[None, :]
    return jnp.tile(jnp.cos(ang), (1, B_HEADS)), jnp.tile(jnp.sin(ang), (1, B_HEADS))


def _unpack_kpe(kpe_rot, bsz, t_len):
    return jnp.concatenate([kpe_rot[:, :ROPE_HALF], kpe_rot[:, LANES:LANES + ROPE_HALF]],
                           axis=-1).reshape(bsz, t_len, QK_ROPE)


def kernel(x_prompt, x_sample, cache_ckv, cache_kpe, page_table, state_shift, state_wkv, state_mlstm_c, state_mlstm_n, state_mlstm_m, w_in_e, mu_e, w0_e, w2_e, a0_e, a2_e, g2_e, kk_e, ka_e, rk_e, lnx_w_e, lnx_b_e, q_norm_e, w_uq_e, kv_norm_e, w_uk_e, w_uv_e, w_out_e, w_in_o, b_i_o, b_f_o, mh_norm_o, w_out_o, peer_wq, peer_k1, peer_k2, peer_u, peer_v, ln1_g, ln1_b, ln2_g, ln2_b):
    bp, tp, _ = x_prompt.shape
    bs, ts, _ = x_sample.shape
    past_len = page_table.shape[1] * PAGE_SIZE
    xp = x_prompt.reshape(bp * tp, D_MODEL)
    xs = x_sample.reshape(bs * ts, D_MODEL)
    rows_s = bs * ts
    tm_s = min(256, rows_s)
    tm_p = 256

    e = 0
    w0 = _layer0_weights(w_in_e[e], mu_e[e], w0_e[e], w2_e[e], a0_e[e], a2_e[e], g2_e[e], kk_e[e],
                         ka_e[e], rk_e[e], lnx_w_e[e], lnx_b_e[e], q_norm_e[e], w_uq_e[e],
                         kv_norm_e[e], w_uk_e[e], w_uv_e[e], w_out_e[e])
    g1, b1 = _row(ln1_g[0]), _row(ln1_b[0])
    cos_p, sin_p = _rope_tables(jnp.arange(tp))
    cos_s, sin_s = _rope_tables(past_len + jnp.arange(ts))
    cos_s = jnp.tile(cos_s, (tm_s // ts, 1))
    sin_s = jnp.tile(sin_s, (tm_s // ts, 1))

    (r, w, k, v, kk, kka, g, bonus, qcat, _, c_p, kpe_p, kcat) = _layer0_prep(
        xp, None, w0, cos_p, sin_p, bp, tp // tm_p, tm_p, tp)
    ch = lambda a: _to_chains(a, bp, tp)
    s0_p = jnp.zeros((A_HD, A_HD, bp * A_HEADS), F32)
    y, s_p = _wkv_scan(ch(r), ch(w), ch(k), ch(v), ch(kk), ch(kka), s0_p, 32)
    y = _from_chains(y, bp, tp)
    out_b = _mla_prompt(qcat, kcat, w0["wuv_pad"], bp, tp, 256)
    xp1 = _layer0_out(y, g, bonus, out_b, xp, w0, g1, b1, tm_p)
    wkv_p = jnp.transpose(s_p.reshape(A_HD, A_HD, bp, A_HEADS), (2, 3, 1, 0))

    start = _mm(state_shift[e], w0["w_in"][:, :RW_PAD], bs)
    start = jnp.repeat(start, ts, axis=0)
    (r, w, k, v, kk, kka, g, bonus, qcat, qpe, c_s, kpe_s, _) = _layer0_prep(
        xs, start, w0, cos_s, sin_s, rows_s // tm_s, 1, tm_s, ts)
    ch = lambda a: _to_chains(a, bs, ts)
    s0_s = jnp.transpose(state_wkv[e], (3, 2, 0, 1)).reshape(A_HD, A_HD, bs * A_HEADS)
    y, s_s = _wkv_scan(ch(r), ch(w), ch(k), ch(v), ch(kk), ch(kka), s0_s, ts)
    y = _from_chains(y, bs, ts)
    q_lat = qcat.reshape(bs, ts * B_HEADS, QCAT)[:, :, :KV_LORA]
    q_pe = jnp.transpose(qpe.reshape(bs, ts, 2, B_HEADS, ROPE_HALF), (0, 1, 3, 2, 4))
    q_pe = q_pe.reshape(bs, ts * B_HEADS, QK_ROPE).astype(BF16)
    kpe_new = _unpack_kpe(kpe_s, bs, ts)
    pad_new = lambda a: jnp.pad(a, ((0, 0), (0, SUBLANES - ts), (0, 0)))
    out_b = _mla_sample(q_lat, q_pe, pad_new(c_s.reshape(bs, ts, KV_LORA)), pad_new(kpe_new),
                        cache_ckv, cache_kpe, page_table, w0["wuv_all"], w0["head_mask"], e, ts)
    xs1 = _layer0_out(y, g, bonus, out_b.reshape(rows_s, B_HEADS * V_HD), xs, w0, g1, b1, tm_s)
    wkv_s = jnp.transpose(s_s.reshape(A_HD, A_HD, bs, A_HEADS), (2, 3, 1, 0))

    new_ckv_prompt = c_p.reshape(1, bp, tp, KV_LORA)
    new_kpe_prompt = _unpack_kpe(kpe_p, bp, tp)[None]
    new_ckv_sample = c_s.reshape(1, bs, ts, KV_LORA)
    new_kpe_sample = kpe_new[None]
    new_shift_prompt = x_prompt[:, -1][None]
    new_shift_sample = x_sample[:, -1][None]

    def peer_weights(l):
        return {"wq": peer_wq[l].astype(BF16), "k1": peer_k1[l].astype(BF16),
                "k2": peer_k2[l].astype(BF16), "u": peer_u[l].astype(BF16),
                "v": peer_v[l].astype(BF16)}

    pw = peer_weights(0)
    g2n, b2n = _row(ln2_g[0]), _row(ln2_b[0])
    xp2 = _peer(xp1, pw, g2n, b2n)
    xs2 = _peer(xs1, pw, g2n, b2n)

    od = 0
    w1 = _layer1_weights(w_in_o[od], b_i_o[od], b_f_o[od], mh_norm_o[od], w_out_o[od])
    g1, b1 = _row(ln1_g[1]), _row(ln1_b[1])
    pad_dk = lambda a: jnp.pad(a, [(0, 0)] * (a.ndim - 1) + [(0, HEAD_PAD - C_DK)])

    q, k, v, o_sig, gates = _layer1_prep(xp2, w1["w_in"], w1["gate_bias"], tm_p)
    zc = jnp.zeros((bp, C_HEADS, C_DV, HEAD_PAD), F32)
    zn = jnp.zeros((bp, C_HEADS, LANES), F32)
    h, c_p1, n_p1, m_p1 = _mlstm(q, k, v, gates, zc, zn, zn, bp, tp, MLSTM_CHUNK)
    xp3 = _layer1_out(h, o_sig, xp2, w1["mh_norm"], w1["w_out"], g1, b1, tm_p)

    q, k, v, o_sig, gates = _layer1_prep(xs2, w1["w_in"], w1["gate_bias"], tm_s)
    lp = MLSTM_CHUNK
    pad_t = lambda a: jnp.pad(a.reshape(bs, ts, -1), ((0, 0), (0, lp - ts), (0, 0))).reshape(
        bs * lp, -1)
    lane = jnp.arange(LANES)
    gate_fill = jnp.where(lane < C_HEADS, GATE_PAD, 0.0).astype(F32)
    gates_p = jnp.concatenate(
        [gates.reshape(bs, ts, LANES), jnp.broadcast_to(gate_fill, (bs, lp - ts, LANES))],
        axis=1).reshape(bs * lp, LANES)
    m0 = jnp.broadcast_to(state_mlstm_m[od][:, :, None], (bs, C_HEADS, LANES))
    h, c_s1, n_s1, m_s1 = _mlstm(pad_t(q), pad_t(k), pad_t(v), gates_p, pad_dk(state_mlstm_c[od]),
                                 pad_dk(state_mlstm_n[od]), m0, bs, lp, lp)
    h = h.reshape(bs, lp, MIX_O)[:, :ts].reshape(rows_s, MIX_O)
    xs3 = _layer1_out(h, o_sig, xs2, w1["mh_norm"], w1["w_out"], g1, b1, tm_s)

    pw = peer_weights(1)
    g2n, b2n = _row(ln2_g[1]), _row(ln2_b[1])
    xp4 = _peer(xp3, pw, g2n, b2n)
    xs4 = _peer(xs3, pw, g2n, b2n)

    return (xp4.reshape(bp, tp, D_MODEL), xs4.reshape(bs, ts, D_MODEL),
            new_ckv_prompt, new_kpe_prompt, new_ckv_sample, new_kpe_sample,
            new_shift_prompt, new_shift_sample, wkv_p[None], wkv_s[None],
            c_p1[..., :C_DK][None], c_s1[..., :C_DK][None],
            n_p1[..., :C_DK][None], n_s1[..., :C_DK][None],
            m_p1[..., 0][None], m_s1[..., 0][None])
```

```python
import functools

import jax
import jax.numpy as jnp
import numpy as np
from jax import lax
from jax.experimental import pallas as pl
from jax.experimental.pallas import tpu as pltpu

F32 = jnp.float32
BF16 = jnp.bfloat16
NEG_INF = float("-inf")

LANES = 128
SUBLANES = 8
VMEM_LIMIT_BYTES = 56 * 1024 * 1024

D_MODEL = 1024
DEPTH = 2
PAGE_SIZE = 128
ALPHA = (2 * DEPTH) ** 0.25
LN_EPS = 1e-5
A_HEADS, A_HD = 8, 64
A_W = A_HEADS * A_HD
LORA_W, LORA_A, LORA_G = 64, 64, 128
RWKV_COLS = 3 * A_W + LORA_W + LORA_A + LORA_G
GN_EPS = 64e-5
B_HEADS, QK_NOPE, QK_ROPE, V_HD = 8, 64, 32, 64
Q_LORA, KV_LORA = 384, 256
SM_SCALE = (QK_NOPE + QK_ROPE) ** -0.5
ROPE_BASE = 10000.0
ROPE_HALF = QK_ROPE // 2
C_HEADS, C_DK, C_DV = 8, 64, 128
MIX_O = C_HEADS * C_DV
MLSTM_CHUNK = 64
GATE_CAP = 15.0
GATE_PAD = -1e30
N_KEYS = 128
N_EXPERTS = N_KEYS * N_KEYS
PEER_HEADS, PEER_QDIM, PEER_TOPK = 8, 256, 16
PEER_HALF = PEER_QDIM // 2
PEER_PAIRS = PEER_HEADS * PEER_TOPK

RW_PAD = 3 * A_W + 3 * LANES
OFF_CQ = RW_PAD
OFF_CKV = OFF_CQ + Q_LORA
OFF_KP1 = OFF_CKV + KV_LORA
OFF_KP2 = OFF_KP1 + LANES
IN_E_PAD = OFF_KP2 + LANES
Q_PAD = B_HEADS * LANES + 2 * LANES
QCAT = 2 * KV_LORA
HEAD_PAD = LANES
OFF_MK = C_HEADS * HEAD_PAD
OFF_MV = 2 * C_HEADS * HEAD_PAD
OFF_MO = OFF_MV + MIX_O
OFF_MG = OFF_MO + MIX_O
IN_O_PAD = OFF_MG + LANES


def _params(*sem):
    return pltpu.CompilerParams(dimension_semantics=sem, vmem_limit_bytes=VMEM_LIMIT_BYTES)


def _dot(a, b):
    return jnp.dot(a, b, preferred_element_type=F32)


def _dot_nt(a, b):
    return lax.dot_general(a, b, (((1,), (1,)), ((), ())), preferred_element_type=F32)


def _dot_tn(a, b):
    return lax.dot_general(a, b, (((0,), (0,)), ((), ())), preferred_element_type=F32)


def _seg_sum(x, seg):
    hi = x.astype(BF16)
    lo = (x - hi.astype(F32)).astype(BF16)
    return _dot(hi, seg) + _dot(lo, seg)


def _layer_norm_res(x, h, g, b):
    z = ALPHA * x + h
    mu = jnp.mean(z, axis=-1, keepdims=True)
    zc = z - mu
    var = jnp.mean(zc * zc, axis=-1, keepdims=True)
    return zc * lax.rsqrt(var + LN_EPS) * g + b


def _full(shape):
    nd = len(shape)
    return pl.BlockSpec(shape, lambda *_: (0,) * nd)


def _mm_kernel(x_ref, w_ref, o_ref):
    o_ref[...] = _dot(x_ref[...].astype(BF16), w_ref[...])


def _mm(x, w, tm):
    m, k = x.shape
    n = w.shape[1]
    return pl.pallas_call(
        _mm_kernel,
        grid=(m // tm,),
        in_specs=[pl.BlockSpec((tm, k), lambda i: (i, 0)), _full((k, n))],
        out_specs=pl.BlockSpec((tm, n), lambda i: (i, 0)),
        out_shape=jax.ShapeDtypeStruct((m, n), F32),
        compiler_params=_params("parallel"),
        name="mm",
    )(x, w)


def _ka_kernel(has_start, seq_len, tm, *refs):
    if has_start:
        x_ref, start_ref = refs[:2]
        refs = refs[2:]
    else:
        x_ref = refs[0]
        start_ref = None
        refs = refs[1:]
    (w_in_ref, mu_ref, w0_ref, w2_ref, a0_ref, a2_ref, g2_ref, kk_ref, ka_ref, rk_ref, qn_ref,
     wuq_ref, kvn_ref, wuk_ref, cos_ref, sin_ref, seg_ref, pem_ref,
     r_out, w_out, k_out, v_out, kk_out, kka_out, g_out, bonus_out, qcat_out, qpe_out, c_out,
     kpe_out, kcat_out, carry_ref) = refs
    t_blk = pl.program_id(1)
    proj = _dot(x_ref[...].astype(BF16), w_in_ref[...])
    p_rw = proj[:, :RW_PAD]
    row = lax.broadcasted_iota(jnp.int32, (tm, 1), 0)
    prev = pltpu.roll(p_rw, 1, 0)
    if has_start:
        prev = jnp.where(row % seq_len == 0, start_ref[...], prev)
    else:
        @pl.when(t_blk == 0)
        def _():
            carry_ref[...] = jnp.zeros_like(carry_ref)

        prev = jnp.where(row == 0, carry_ref[SUBLANES - 1:SUBLANES, :], prev)
        carry_ref[...] = p_rw[tm - SUBLANES:, :]
    p = p_rw + (prev - p_rw) * mu_ref[...]
    r = p[:, 0:A_W]
    k = p[:, A_W:2 * A_W]
    v = p[:, 2 * A_W:3 * A_W]
    lw = p[:, 3 * A_W:3 * A_W + LANES]
    la = p[:, 3 * A_W + LANES:3 * A_W + 2 * LANES]
    lg = p[:, 3 * A_W + 2 * LANES:RW_PAD]
    w_pre = w0_ref[...] + _dot(jnp.tanh(lw).astype(BF16), w2_ref[...])
    w_log = -jax.nn.softplus(-w_pre) - 0.5
    decay = jnp.exp(-jnp.exp(w_log))
    a = jax.nn.sigmoid(a0_ref[...] + _dot(la.astype(BF16), a2_ref[...]))
    g = _dot(jax.nn.sigmoid(lg).astype(BF16), g2_ref[...])
    seg = seg_ref[...]
    kk = k * kk_ref[...]
    kk = kk / jnp.maximum(jnp.sqrt(_seg_sum(kk * kk, seg)), 1e-12)
    k = k * (1.0 + (a - 1.0) * ka_ref[...])
    r_out[...] = r
    w_out[...] = decay
    k_out[...] = k
    v_out[...] = v
    kk_out[...] = kk
    kka_out[...] = kk * a
    g_out[...] = g
    bonus_out[...] = _seg_sum(r * k * rk_ref[...], seg) * v
    cq = proj[:, OFF_CQ:OFF_CKV]
    ckv = proj[:, OFF_CKV:OFF_KP1]
    kp1 = proj[:, OFF_KP1:OFF_KP2]
    kp2 = proj[:, OFF_KP2:IN_E_PAD]
    cos = cos_ref[...]
    sin = sin_ref[...]
    cqn = cq * lax.rsqrt(jnp.mean(cq * cq, axis=-1, keepdims=True) + 1e-6) * qn_ref[...]
    q = _dot(cqn.astype(BF16), wuq_ref[...])
    x1 = q[:, B_HEADS * LANES:B_HEADS * LANES + LANES]
    x2 = q[:, B_HEADS * LANES + LANES:Q_PAD]
    qpe = jnp.concatenate([x1 * cos - x2 * sin, x1 * sin + x2 * cos], axis=-1)
    qpe_out[...] = qpe
    for h in range(B_HEADS):
        q_lat = _dot(q[:, h * LANES:(h + 1) * LANES].astype(BF16), wuk_ref[h])
        qcat_out[:, h * QCAT:h * QCAT + KV_LORA] = q_lat.astype(BF16)
        qcat_out[:, h * QCAT + KV_LORA:(h + 1) * QCAT] = (qpe * pem_ref[h:h + 1, :]).astype(BF16)
    c = ckv * lax.rsqrt(jnp.mean(ckv * ckv, axis=-1, keepdims=True) + 1e-6) * kvn_ref[...]
    kpe = jnp.concatenate([kp1 * cos - kp2 * sin, kp1 * sin + kp2 * cos], axis=-1)
    c_out[...] = c
    kpe_out[...] = kpe
    kcat_out[:, :KV_LORA] = c.astype(BF16)
    kcat_out[:, KV_LORA:] = kpe.astype(BF16)


def _layer0_prep(x, start, wts, cos, sin, n_seq_blocks, n_t_blocks, tm, seq_len):
    rows = x.shape[0]
    has_start = start is not None
    rmap = lambda b, t: (b * n_t_blocks + t, 0)
    tmap = lambda b, t: (t, 0)
    ins = [x] + ([start] if has_start else [])
    in_specs = [pl.BlockSpec((tm, D_MODEL), rmap)]
    if has_start:
        in_specs.append(pl.BlockSpec((tm, RW_PAD), rmap))
    w_names = ("w_in", "mu", "w0", "w2", "a0", "a2", "g2", "kk", "ka", "rk", "q_norm", "w_uq",
               "kv_norm", "w_uk")
    for nme in w_names:
        ins.append(wts[nme])
        in_specs.append(_full(wts[nme].shape))
    ins += [cos, sin, wts["seg"], wts["pe_mask"]]
    in_specs += [pl.BlockSpec((tm, LANES), tmap), pl.BlockSpec((tm, LANES), tmap),
                 _full(wts["seg"].shape), _full(wts["pe_mask"].shape)]
    widths = [A_W] * 8 + [B_HEADS * QCAT, 2 * LANES, KV_LORA, 2 * LANES, QCAT]
    dtypes = [F32] * 8 + [BF16, F32, F32, F32, BF16]
    out_shape = [jax.ShapeDtypeStruct((rows, wd), dt) for wd, dt in zip(widths, dtypes)]
    out_specs = [pl.BlockSpec((tm, wd), rmap) for wd in widths]
    return pl.pallas_call(
        functools.partial(_ka_kernel, has_start, seq_len, tm),
        grid=(n_seq_blocks, n_t_blocks),
        in_specs=in_specs,
        out_specs=out_specs,
        out_shape=out_shape,
        scratch_shapes=[pltpu.VMEM((SUBLANES, RW_PAD), F32)],
        compiler_params=_params("parallel", "arbitrary"),
        name="layer0_prep",
    )(*ins)


def _wkv_kernel(tch, r_ref, w_ref, k_ref, v_ref, kk_ref, kka_ref, s0_ref, y_ref, sout_ref,
                s_ref, sa_ref):
    tc = pl.program_id(1)

    @pl.when(tc == 0)
    def _():
        s_ref[...] = s0_ref[...]

    acc = jnp.zeros((A_HD, LANES), F32)
    for j in range(A_HD):
        acc = acc + s_ref[j] * kk_ref[0, pl.ds(j, 1), :]
    sa_ref[...] = -acc

    def step(t, carry):
        tn = jnp.minimum(t + 1, tch - 1)
        sa = sa_ref[...]
        v_t = v_ref[t]
        y = jnp.zeros((A_HD, LANES), F32)
        san = jnp.zeros((A_HD, LANES), F32)
        for j in range(A_HD):
            row = pl.ds(j, 1)
            sj = s_ref[j] * w_ref[t, row, :] + sa * kka_ref[t, row, :] + v_t * k_ref[t, row, :]
            s_ref[j] = sj
            y = y + sj * r_ref[t, row, :]
            san = san + sj * kk_ref[tn, row, :]
        y_ref[t] = y
        sa_ref[...] = -san
        return carry

    lax.fori_loop(0, tch, step, 0)

    @pl.when(tc == pl.num_programs(1) - 1)
    def _():
        sout_ref[...] = s_ref[...]


def _wkv_scan(r, w, k, v, kk, kka, s0, tch):
    t_len, _, n_chains = r.shape
    pad = (-n_chains) % LANES
    if pad:
        padc = lambda a: jnp.pad(a, ((0, 0), (0, 0), (0, pad)))
        y, s = _wkv_scan(*(padc(a) for a in (r, w, k, v, kk, kka, s0)), tch)
        return y[..., :n_chains], s[..., :n_chains]
    chains = n_chains
    seq = pl.BlockSpec((tch, A_HD, LANES), lambda c, t: (t, 0, c))
    st = pl.BlockSpec((A_HD, A_HD, LANES), lambda c, t: (0, 0, c))
    return pl.pallas_call(
        functools.partial(_wkv_kernel, tch),
        grid=(chains // LANES, t_len // tch),
        in_specs=[seq] * 6 + [st],
        out_specs=[seq, st],
        out_shape=[jax.ShapeDtypeStruct((t_len, A_HD, chains), F32),
                   jax.ShapeDtypeStruct((A_HD, A_HD, chains), F32)],
        scratch_shapes=[pltpu.VMEM((A_HD, A_HD, LANES), F32), pltpu.VMEM((A_HD, LANES), F32)],
        compiler_params=_params("parallel", "arbitrary"),
        name="wkv_scan",
    )(r, w, k, v, kk, kka, s0)


def _to_chains(x, bsz, t_len):
    x = x.reshape(bsz, t_len, A_HEADS, A_HD)
    return jnp.transpose(x, (1, 3, 0, 2)).reshape(t_len, A_HD, bsz * A_HEADS)


def _from_chains(y, bsz, t_len):
    y = y.reshape(t_len, A_HD, bsz, A_HEADS)
    return jnp.transpose(y, (2, 0, 3, 1)).reshape(bsz * t_len, A_W)


def _kb_kernel(y_ref, g_ref, bonus_ref, ob_ref, x_ref, lw_ref, lb_ref, seg_ref, wo_ref, ng_ref,
               nb_ref, o_ref):
    seg = seg_ref[...]
    y = y_ref[...]
    mu = _seg_sum(y, seg) * (1.0 / A_HD)
    yc = y - mu
    var = _seg_sum(yc * yc, seg) * (1.0 / A_HD)
    y = yc * lax.rsqrt(var + GN_EPS) * lw_ref[...] + lb_ref[...] + bonus_ref[...]
    out_a = (y * g_ref[...]).astype(BF16)
    h = _dot(out_a, wo_ref[:A_W, :]) + _dot(ob_ref[...].astype(BF16), wo_ref[A_W:, :])
    o_ref[...] = _layer_norm_res(x_ref[...], h, ng_ref[...], nb_ref[...])


def _layer0_out(y, g, bonus, out_b, x, wts, ln_g, ln_b, tm):
    rows = x.shape[0]
    half = pl.BlockSpec((tm, A_W), lambda i: (i, 0))
    full = pl.BlockSpec((tm, D_MODEL), lambda i: (i, 0))
    consts = [wts["lnx_w"], wts["lnx_b"], wts["seg"], wts["w_out"], ln_g, ln_b]
    return pl.pallas_call(
        _kb_kernel,
        grid=(rows // tm,),
        in_specs=[half, half, half, half, full] + [_full(c.shape) for c in consts],
        out_specs=full,
        out_shape=jax.ShapeDtypeStruct((rows, D_MODEL), F32),
        compiler_params=_params("parallel"),
        name="layer0_out",
    )(y, g, bonus, out_b, x, *consts)


def _lane_tiles(x):
    return [x[:, j * LANES:(j + 1) * LANES] for j in range(x.shape[1] // LANES)]


def _mla_p_kernel(tq, q_ref, kc_ref, wuv_ref, o_ref, m_ref, l_ref, acc_ref):
    qi = pl.program_id(1)
    rr = lax.broadcasted_iota(jnp.int32, (tq, tq), 0)
    cc = lax.broadcasted_iota(jnp.int32, (tq, tq), 1)
    causal = cc <= rr
    m_ref[...] = jnp.full(m_ref.shape, NEG_INF, F32)
    l_ref[...] = jnp.zeros(l_ref.shape, F32)
    acc_ref[...] = jnp.zeros(acc_ref.shape, F32)

    def chunk(c, mask):
        kc = kc_ref[pl.ds(pl.multiple_of(c * tq, tq), tq), :]

        def scores(h):
            return _dot_nt(q_ref[:, h * QCAT:(h + 1) * QCAT], kc)

        lane_tiles = _lane_tiles
        s_next = scores(0)
        for h in range(B_HEADS):
            s = s_next * SM_SCALE
            if h + 1 < B_HEADS:
                s_next = scores(h + 1)
            if mask:
                s = jnp.where(causal, s, NEG_INF)
            m = m_ref[h]
            s_tiles = lane_tiles(s)
            s_max = functools.reduce(jnp.maximum, s_tiles)
            m_new = jnp.maximum(m, jnp.max(s_max, axis=-1, keepdims=True))
            alpha = jnp.exp(m - m_new)
            p_tiles = [jnp.exp(st - m_new) for st in s_tiles]
            p_sum = functools.reduce(jnp.add, p_tiles)
            m_ref[h] = m_new
            l_ref[h] = alpha * l_ref[h] + jnp.sum(p_sum, axis=-1, keepdims=True)
            p = jnp.concatenate(p_tiles, axis=-1).astype(BF16)
            pv = _dot(p, kc[:, :KV_LORA])
            acc_ref[h] = jnp.concatenate(
                [alpha * a + o for a, o in zip(lane_tiles(acc_ref[h]), lane_tiles(pv))], axis=-1)

    def body(c, carry):
        chunk(c, False)
        return carry

    lax.fori_loop(0, qi, body, 0)
    chunk(qi, True)
    out = jnp.zeros((tq, B_HEADS * V_HD), F32)
    for h in range(B_HEADS):
        l = l_ref[h]
        o_lat = jnp.concatenate([a / l for a in _lane_tiles(acc_ref[h])], axis=-1)
        out = out + _dot(o_lat.astype(BF16), wuv_ref[h])
    o_ref[...] = out


def _mla_prompt(qcat, kcat, wuv_pad, bsz, t_len, tq):
    nq = t_len // tq
    return pl.pallas_call(
        functools.partial(_mla_p_kernel, tq),
        grid=(bsz, nq),
        in_specs=[pl.BlockSpec((tq, B_HEADS * QCAT), lambda b, i: (b * nq + i, 0)),
                  pl.BlockSpec((t_len, QCAT), lambda b, i: (b, 0)),
                  _full(wuv_pad.shape)],
        out_specs=pl.BlockSpec((tq, B_HEADS * V_HD), lambda b, i: (b * nq + i, 0)),
        out_shape=jax.ShapeDtypeStruct((bsz * t_len, B_HEADS * V_HD), F32),
        scratch_shapes=[pltpu.VMEM((B_HEADS, tq, LANES), F32),
                        pltpu.VMEM((B_HEADS, tq, LANES), F32),
                        pltpu.VMEM((B_HEADS, tq, KV_LORA), F32)],
        compiler_params=_params("parallel", "arbitrary"),
        name="mla_prompt",
    )(qcat, kcat, wuv_pad)


PAGES_PER_STEP = 64


def _mla_s_kernel(n_new, pt_ref, ql_ref, qp_ref, cn_ref, kn_ref, *rest):
    ck_refs = rest[:PAGES_PER_STEP]
    kp_refs = rest[PAGES_PER_STEP:2 * PAGES_PER_STEP]
    wuv_ref, hm_ref, o_ref, m_ref, l_ref, acc_ref = rest[2 * PAGES_PER_STEP:]
    step = pl.program_id(1)
    ql = ql_ref[...]
    qp = qp_ref[...]
    n_rows = ql.shape[0]

    @pl.when(step == 0)
    def _():
        cn = cn_ref[...].astype(BF16)
        kn = kn_ref[...].astype(BF16)
        s = (_dot_nt(ql, cn) + _dot_nt(qp, kn)) * SM_SCALE
        tok = lax.broadcasted_iota(jnp.int32, s.shape, 0) // B_HEADS
        col = lax.broadcasted_iota(jnp.int32, s.shape, 1)
        s = jnp.where(col <= tok, s, NEG_INF)
        m = jnp.max(s, axis=-1, keepdims=True)
        p = jnp.exp(s - m)
        m_ref[...] = m
        l_ref[...] = jnp.sum(p, axis=-1, keepdims=True)
        acc_ref[...] = _dot(p.astype(BF16), cn)

    cb = jnp.concatenate([r[...].astype(BF16) for r in ck_refs], axis=0)
    kb = jnp.concatenate([r[...].astype(BF16) for r in kp_refs], axis=1)
    s = (_dot_nt(ql, cb) + _dot(qp, kb)) * SM_SCALE
    m = m_ref[...]
    m_new = jnp.maximum(m, jnp.max(s, axis=-1, keepdims=True))
    alpha = jnp.exp(m - m_new)
    p = jnp.exp(s - m_new)
    m_ref[...] = m_new
    l_ref[...] = alpha * l_ref[...] + jnp.sum(p, axis=-1, keepdims=True)
    acc_ref[...] = alpha * acc_ref[...] + _dot(p.astype(BF16), cb)

    @pl.when(step == pl.num_programs(1) - 1)
    def _():
        o = (acc_ref[...] / l_ref[...]).astype(BF16)
        full = _dot(o, wuv_ref[...])
        full = full.reshape(n_new, B_HEADS, B_HEADS * V_HD) * hm_ref[...]
        o_ref[...] = jnp.sum(full, axis=1)


def _mla_sample(q_lat, q_pe, c_new, k_new, cache_ckv, cache_kpe, page_table, wuv_all, head_mask,
                layer, n_new):
    n_seq, n_pages = page_table.shape
    n_steps = n_pages // PAGES_PER_STEP
    n_rows = q_lat.shape[1]

    def page_spec(shape, k):
        return pl.BlockSpec((None, None) + shape,
                            lambda b, s, pt: (layer, pt[b, s * PAGES_PER_STEP + k], 0, 0))

    in_specs = [pl.BlockSpec((None, n_rows, KV_LORA), lambda b, s, pt: (b, 0, 0)),
                pl.BlockSpec((None, n_rows, QK_ROPE), lambda b, s, pt: (b, 0, 0)),
                pl.BlockSpec((None, SUBLANES, KV_LORA), lambda b, s, pt: (b, 0, 0)),
                pl.BlockSpec((None, SUBLANES, QK_ROPE), lambda b, s, pt: (b, 0, 0))]
    in_specs += [page_spec((PAGE_SIZE, KV_LORA), k) for k in range(PAGES_PER_STEP)]
    in_specs += [page_spec((QK_ROPE, PAGE_SIZE), k) for k in range(PAGES_PER_STEP)]
    in_specs += [pl.BlockSpec(wuv_all.shape, lambda b, s, pt: (0, 0)),
                 pl.BlockSpec(head_mask.shape, lambda b, s, pt: (0, 0))]
    grid_spec = pltpu.PrefetchScalarGridSpec(
        num_scalar_prefetch=1,
        grid=(n_seq, n_steps),
        in_specs=in_specs,
        out_specs=pl.BlockSpec((None, n_new, B_HEADS * V_HD), lambda b, s, pt: (b, 0, 0)),
        scratch_shapes=[pltpu.VMEM((n_rows, 1), F32), pltpu.VMEM((n_rows, 1), F32),
                        pltpu.VMEM((n_rows, KV_LORA), F32)],
    )
    return pl.pallas_call(
        functools.partial(_mla_s_kernel, n_new),
        grid_spec=grid_spec,
        out_shape=jax.ShapeDtypeStruct((n_seq, n_new, B_HEADS * V_HD), F32),
        compiler_params=_params("parallel", "arbitrary"),
        name="mla_sample",
    )(page_table, q_lat, q_pe, c_new, k_new, *([cache_ckv] * PAGES_PER_STEP),
      *([jnp.swapaxes(cache_kpe, 2, 3)] * PAGES_PER_STEP), wuv_all, head_mask)


def _kc_kernel(x_ref, w_ref, gb_ref, q_out, k_out, v_out, o_out, g_out):
    proj = _dot(x_ref[...].astype(BF16), w_ref[...])
    q_out[...] = (proj[:, :OFF_MK] * (C_DK ** -0.5)).astype(BF16)
    k_out[...] = proj[:, OFF_MK:OFF_MV].astype(BF16)
    v_out[...] = proj[:, OFF_MV:OFF_MO].astype(BF16)
    o_out[...] = jax.nn.sigmoid(proj[:, OFF_MO:OFF_MG])
    pre = proj[:, OFF_MG:] + gb_ref[...]
    cap = GATE_CAP * jnp.tanh(pre / GATE_CAP)
    lane = lax.broadcasted_iota(jnp.int32, cap.shape, 1)
    g_out[...] = jnp.where(lane < C_HEADS, cap, jax.nn.log_sigmoid(cap))


def _layer1_prep(x, w_in, gate_bias, tm):
    rows = x.shape[0]
    widths = [OFF_MK, OFF_MK, MIX_O, MIX_O, LANES]
    dtypes = [BF16, BF16, BF16, F32, F32]
    return pl.pallas_call(
        _kc_kernel,
        grid=(rows // tm,),
        in_specs=[pl.BlockSpec((tm, D_MODEL), lambda i: (i, 0)), _full(w_in.shape),
                  _full(gate_bias.shape)],
        out_specs=[pl.BlockSpec((tm, wd), lambda i: (i, 0)) for wd in widths],
        out_shape=[jax.ShapeDtypeStruct((rows, wd), dt) for wd, dt in zip(widths, dtypes)],
        compiler_params=_params("parallel"),
        name="layer1_prep",
    )(x, w_in, gate_bias)


def _mlstm_kernel(chunk, q_ref, k_ref, v_ref, g_ref, c0_ref, n0_ref, m0_ref, h_ref, co_ref,
                  no_ref, mo_ref, c_s, n_s, m_s):
    ci = pl.program_id(1)

    @pl.when(ci == 0)
    def _():
        c_s[...] = c0_ref[...]
        n_s[...] = n0_ref[...]
        m_s[...] = m0_ref[...]

    g = g_ref[...]
    row = lax.broadcasted_iota(jnp.int32, g.shape, 0)
    lane = lax.broadcasted_iota(jnp.int32, g.shape, 1)
    b = g
    shift = 1
    while shift < chunk:
        b = b + jnp.where(row >= shift, pltpu.roll(b, shift, 0), 0.0)
        shift *= 2
    g_t = g.T
    b_t = b.T
    rr = lax.broadcasted_iota(jnp.int32, (chunk, chunk), 0)
    cc = lax.broadcasted_iota(jnp.int32, (chunk, chunk), 1)
    causal = cc <= rr
    for h in range(C_HEADS):
        b_col = jnp.sum(jnp.where(lane == C_HEADS + h, b, 0.0), axis=-1, keepdims=True)
        i_col = jnp.sum(jnp.where(lane == h, g, 0.0), axis=-1, keepdims=True)
        b_row = b_t[C_HEADS + h:C_HEADS + h + 1, :]
        i_row = g_t[h:h + 1, :]
        m_prev = m_s[h:h + 1, 0:1]
        d = jnp.where(causal, b_col - b_row + i_row, NEG_INF)
        gg = b_col + m_prev
        mt = jnp.maximum(gg, jnp.max(d, axis=-1, keepdims=True))
        w_inter = jnp.exp(gg - mt)
        qh = q_ref[:, h * HEAD_PAD:(h + 1) * HEAD_PAD]
        kh = k_ref[:, h * HEAD_PAD:(h + 1) * HEAD_PAD]
        vh = v_ref[:, h * C_DV:(h + 1) * C_DV]
        c_prev = c_s[h]
        n_prev = n_s[h:h + 1, :]
        att = jnp.exp(d - mt) * _dot_nt(qh, kh)
        num = w_inter * _dot_nt(qh, c_prev.astype(BF16)) + _dot(att.astype(BF16), vh)
        qn = jnp.sum(qh.astype(F32) * n_prev, axis=-1, keepdims=True)
        den = w_inter * qn + jnp.sum(att, axis=-1, keepdims=True)
        h_ref[:, h * C_DV:(h + 1) * C_DV] = num / jnp.maximum(jnp.abs(den), jnp.exp(-mt))
        m_new = mt[chunk - 1:chunk, :]
        b_last = b_col[chunk - 1:chunk, :]
        w_state = jnp.exp(b_last - b_col + i_col - m_new)
        dec = jnp.exp(b_last + m_prev - m_new)
        wv = (w_state * vh.astype(F32)).astype(BF16)
        c_s[h] = dec * c_prev + _dot_tn(wv, kh)
        n_s[h:h + 1, :] = dec * n_prev + jnp.sum(w_state * kh.astype(F32), axis=0, keepdims=True)
        m_s[h:h + 1, :] = jnp.broadcast_to(m_new, (1, LANES))

    @pl.when(ci == pl.num_programs(1) - 1)
    def _():
        co_ref[...] = c_s[...]
        no_ref[...] = n_s[...]
        mo_ref[...] = m_s[...]


def _mlstm(q, k, v, gates, c0, n0, m0, bsz, t_len, chunk):
    nc = t_len // chunk
    seq = lambda wd: pl.BlockSpec((chunk, wd), lambda b, c: (b * nc + c, 0))
    c_spec = pl.BlockSpec((None, C_HEADS, C_DV, HEAD_PAD), lambda b, c: (b, 0, 0, 0))
    v_spec = pl.BlockSpec((None, C_HEADS, LANES), lambda b, c: (b, 0, 0))
    return pl.pallas_call(
        functools.partial(_mlstm_kernel, chunk),
        grid=(bsz, nc),
        in_specs=[seq(OFF_MK), seq(OFF_MK), seq(MIX_O), seq(LANES), c_spec, v_spec, v_spec],
        out_specs=[seq(MIX_O), c_spec, v_spec, v_spec],
        out_shape=[jax.ShapeDtypeStruct((bsz * t_len, MIX_O), F32),
                   jax.ShapeDtypeStruct(c0.shape, F32),
                   jax.ShapeDtypeStruct(n0.shape, F32),
                   jax.ShapeDtypeStruct(m0.shape, F32)],
        scratch_shapes=[pltpu.VMEM((C_HEADS, C_DV, HEAD_PAD), F32),
                        pltpu.VMEM((C_HEADS, LANES), F32), pltpu.VMEM((C_HEADS, LANES), F32)],
        compiler_params=_params("parallel", "arbitrary"),
        name="mlstm",
    )(q, k, v, gates, c0, n0, m0)


def _kd_kernel(h_ref, o_ref, x_ref, mh_ref, wo_ref, ng_ref, nb_ref, out_ref):
    parts = []
    for h in range(C_HEADS):
        hh = h_ref[:, h * C_DV:(h + 1) * C_DV]
        parts.append(hh * lax.rsqrt(jnp.mean(hh * hh, axis=-1, keepdims=True) + 1e-6))
    hn = jnp.concatenate(parts, axis=-1) * mh_ref[...]
    y = _dot((hn * o_ref[...]).astype(BF16), wo_ref[...])
    out_ref[...] = _layer_norm_res(x_ref[...], y, ng_ref[...], nb_ref[...])


def _layer1_out(h, o_sig, x, mh_norm, w_out, ln_g, ln_b, tm):
    rows = x.shape[0]
    full = pl.BlockSpec((tm, D_MODEL), lambda i: (i, 0))
    consts = [mh_norm, w_out, ln_g, ln_b]
    return pl.pallas_call(
        _kd_kernel,
        grid=(rows // tm,),
        in_specs=[full, full, full] + [_full(c.shape) for c in consts],
        out_specs=full,
        out_shape=jax.ShapeDtypeStruct((rows, D_MODEL), F32),
        compiler_params=_params("parallel"),
        name="layer1_out",
    )(h, o_sig, x, *consts)


def _top16(s_ref, val_ref, idx_ref):
    n = s_ref.shape[0]
    row = lax.broadcasted_iota(jnp.int32, s_ref.shape, 0)
    s = s_ref[...]
    for r in range(PEER_TOPK):
        m = jnp.max(s, axis=0, keepdims=True)
        am = jnp.min(jnp.where(s == m, row, n), axis=0, keepdims=True)
        val_ref[r:r + 1, :] = m
        idx_ref[r:r + 1, :] = am
        s = jnp.where(row == am, NEG_INF, s)


def _sort16_network():
    n, pairs, p = PEER_TOPK, [], 1
    while p < n:
        k = p
        while k >= 1:
            for j in range(k % p, n - k, 2 * k):
                for i in range(min(k, n - j - k)):
                    if (i + j) // (2 * p) == (i + j + k) // (2 * p):
                        pairs.append((i + j, i + j + k))
            k //= 2
        p *= 2
    return pairs


SORT16 = _sort16_network()


def _top16_keys(s_ref, val_ref, idx_ref):
    n_slab = N_KEYS // SUBLANES
    row = lax.broadcasted_iota(jnp.int32, (SUBLANES, LANES), 0)
    for g in range(s_ref.shape[1] // LANES):
        cols = slice(g * LANES, (g + 1) * LANES)
        vals = [s_ref[v * SUBLANES:(v + 1) * SUBLANES, cols] for v in range(n_slab)]
        keys = [row + v * SUBLANES for v in range(n_slab)]
        for a, b in SORT16:
            keep = (vals[a] > vals[b]) | ((vals[a] == vals[b]) & (keys[a] < keys[b]))
            vals[a], vals[b] = (jnp.where(keep, vals[a], vals[b]),
                                jnp.where(keep, vals[b], vals[a]))
            keys[a], keys[b] = (jnp.where(keep, keys[a], keys[b]),
                                jnp.where(keep, keys[b], keys[a]))
        for r in range(PEER_TOPK):
            m = jnp.max(vals[0], axis=0, keepdims=True)
            am = jnp.min(jnp.where(vals[0] == m, keys[0], N_KEYS), axis=0, keepdims=True)
            val_ref[r:r + 1, cols] = m
            idx_ref[r:r + 1, cols] = am
            taken = keys[0] == am
            for v in range(PEER_TOPK - 1 - r):
                vals[v] = jnp.where(taken, vals[v + 1], vals[v])
                keys[v] = jnp.where(taken, keys[v + 1], keys[v])


def _pick_rank(src_ref, rank):
    out = jnp.zeros(rank.shape, jnp.int32)
    for r in range(PEER_TOPK):
        out = jnp.where(rank == r, src_ref[r:r + 1, :], out)
    return out


CAND_COUNT = [PEER_TOPK // (i + 1) for i in range(PEER_TOPK)]
CAND_START = [sum(CAND_COUNT[:i]) for i in range(PEER_TOPK + 1)]
CAND_ROWS = -(-CAND_START[PEER_TOPK] // SUBLANES) * SUBLANES


def _peer_q_kernel(tb, x_ref, wq_ref, k1_ref, k2_ref, i1_out, i2_out, g_out,
                   s_ref, cand_ref, t1_ref, j1_ref, t2_ref, j2_ref, tc_ref, jc_ref,
                   e1_ref, e2_ref, gt_ref):
    q = _dot(x_ref[...].astype(BF16), wq_ref[...])
    for h in range(PEER_HEADS):
        q1 = q[:, h * PEER_QDIM:h * PEER_QDIM + PEER_HALF].astype(BF16)
        q2 = q[:, h * PEER_QDIM + PEER_HALF:(h + 1) * PEER_QDIM].astype(BF16)
        s_ref[...] = _dot_nt(k1_ref[...], q1)
        _top16_keys(s_ref, t1_ref, j1_ref)
        s_ref[...] = _dot_nt(k2_ref[...], q2)
        _top16_keys(s_ref, t2_ref, j2_ref)
        for i in range(PEER_TOPK):
            lo, n = CAND_START[i], CAND_COUNT[i]
            cand_ref[lo:lo + n, :] = t1_ref[i:i + 1, :] + t2_ref[0:n, :]
        cand_ref[CAND_START[PEER_TOPK]:, :] = jnp.full(
            (CAND_ROWS - CAND_START[PEER_TOPK], tb), NEG_INF, F32)
        _top16(cand_ref, tc_ref, jc_ref)
        pos = jc_ref[...]
        rank1 = jnp.zeros_like(pos)
        start = jnp.zeros_like(pos)
        for i in range(1, PEER_TOPK):
            past = pos >= CAND_START[i]
            rank1 = rank1 + past.astype(jnp.int32)
            start = jnp.where(past, CAND_START[i], start)
        rows = slice(h * PEER_TOPK, (h + 1) * PEER_TOPK)
        e1_ref[rows, :] = _pick_rank(j1_ref, rank1).astype(F32)
        e2_ref[rows, :] = _pick_rank(j2_ref, pos - start).astype(F32)
        top = tc_ref[...]
        e = jnp.exp(top - top[0:1, :])
        gt_ref[rows, :] = e / jnp.sum(e, axis=0, keepdims=True)
    i1_out[...] = e1_ref[...].T.astype(jnp.int32)
    i2_out[...] = e2_ref[...].T.astype(jnp.int32)
    g_out[...] = gt_ref[...].T


def _peer_select(x, wq, k1, k2, tb):
    rows = x.shape[0]
    out = pl.BlockSpec((tb, PEER_PAIRS), lambda i: (i, 0))
    small = lambda dt: pltpu.VMEM((PEER_TOPK, tb), dt)
    return pl.pallas_call(
        functools.partial(_peer_q_kernel, tb),
        grid=(rows // tb,),
        in_specs=[pl.BlockSpec((tb, D_MODEL), lambda i: (i, 0)), _full(wq.shape), _full(k1.shape),
                  _full(k2.shape)],
        out_specs=[out, out, out],
        out_shape=[jax.ShapeDtypeStruct((rows, PEER_PAIRS), jnp.int32),
                   jax.ShapeDtypeStruct((rows, PEER_PAIRS), jnp.int32),
                   jax.ShapeDtypeStruct((rows, PEER_PAIRS), F32)],
        scratch_shapes=[pltpu.VMEM((N_KEYS, tb), F32),
                        pltpu.VMEM((CAND_ROWS, tb), F32),
                        small(F32), small(jnp.int32), small(F32), small(jnp.int32),
                        small(F32), small(jnp.int32),
                        pltpu.VMEM((PEER_PAIRS, tb), F32), pltpu.VMEM((PEER_PAIRS, tb), F32),
                        pltpu.VMEM((PEER_PAIRS, tb), F32)],
        compiler_params=_params("parallel"),
        name="peer_select",
    )(x, wq, k1, k2)


PEER_SPLIT = 2
PAIR_UNROLL = 8


def _peer_u_kernel(tb, n_i1, x_ref, u_ref, i1_ref, i2_ref, g_ref, a_ref, hs_ref, gs_ref, xb_ref):
    part = pl.program_id(0)
    n_pairs = n_i1 // 2
    tokens_per_pair = tb // n_pairs
    xb_ref[...] = x_ref[...].astype(BF16)
    sub1 = lax.broadcasted_iota(jnp.int32, (n_i1, PEER_PAIRS), 0) + part * n_i1
    sub2 = lax.broadcasted_iota(jnp.int32, (N_KEYS, PEER_PAIRS), 0)

    def slab_pair(pp):
        u2 = u_ref[pl.ds(pl.multiple_of(pp * 2 * N_KEYS, 2 * N_KEYS), 2 * N_KEYS), :]
        h2 = _dot_nt(xb_ref[...], u2)
        act = 0.5 * h2 * (1.0 + lax.erf(h2 * (2.0 ** -0.5)))
        base = pl.multiple_of(pp * 2 * tb, 2 * tb)
        hs_ref[pl.ds(base, tb), :] = act[:, :N_KEYS]
        hs_ref[pl.ds(base + tb, tb), :] = act[:, N_KEYS:]

    def gate_rows(t):
        i1 = i1_ref[pl.ds(t, 1), :]
        i2 = i2_ref[pl.ds(t, 1), :]
        gate = g_ref[pl.ds(t, 1), :]
        sel1 = jnp.where(sub1 == i1, gate, 0.0).astype(BF16)
        sel2 = jnp.where(sub2 == i2, 1.0, 0.0).astype(BF16)
        gs_ref[pl.ds(t, n_i1, stride=tb), :] = _dot_nt(sel1, sel2)

    def pair(pp, carry):
        slab_pair(pp)
        for k in range(tokens_per_pair):
            gate_rows(pp * tokens_per_pair + k)
        return carry

    lax.fori_loop(0, n_pairs, pair, 0, unroll=PAIR_UNROLL)
    for i in range(n_i1):
        rows = slice(i * tb, (i + 1) * tb)
        a_ref[:, i * N_KEYS:(i + 1) * N_KEYS] = (hs_ref[rows, :] * gs_ref[rows, :]).astype(BF16)


def _peer_act(x, u, i1, i2, gate, tb):
    rows = x.shape[0]
    n_i1 = N_KEYS // PEER_SPLIT
    n_exp = N_EXPERTS // PEER_SPLIT
    assert tb % (n_i1 // 2) == 0
    tok = lambda wd: pl.BlockSpec((tb, wd), lambda p, i: (i, 0))
    slab = pltpu.VMEM((n_i1 * tb, N_KEYS), F32)
    return pl.pallas_call(
        functools.partial(_peer_u_kernel, tb, n_i1),
        grid=(PEER_SPLIT, rows // tb),
        in_specs=[tok(D_MODEL),
                  pl.BlockSpec((n_exp, D_MODEL), lambda p, i: (p, 0),
                               pipeline_mode=pl.Buffered(1)),
                  tok(PEER_PAIRS), tok(PEER_PAIRS), tok(PEER_PAIRS)],
        out_specs=pl.BlockSpec((tb, n_exp), lambda p, i: (i, p)),
        out_shape=jax.ShapeDtypeStruct((rows, N_EXPERTS), BF16),
        scratch_shapes=[slab, slab, pltpu.VMEM((tb, D_MODEL), BF16)],
        compiler_params=_params("arbitrary", "arbitrary"),
        name="peer_act",
    )(x, u, i1, i2, gate)


def _peer_v_kernel(a_ref, v_ref, x_ref, ng_ref, nb_ref, o_ref, acc_ref):
    kk = pl.program_id(1)

    @pl.when(kk == 0)
    def _():
        acc_ref[...] = jnp.zeros_like(acc_ref)

    acc_ref[...] += _dot(a_ref[...], v_ref[...])

    @pl.when(kk == pl.num_programs(1) - 1)
    def _():
        o_ref[...] = _layer_norm_res(x_ref[...], acc_ref[...], ng_ref[...], nb_ref[...])


def _peer_out(a, v, x, ln_g, ln_b, tm, tk):
    rows = x.shape[0]
    return pl.pallas_call(
        _peer_v_kernel,
        grid=(rows // tm, N_EXPERTS // tk),
        in_specs=[pl.BlockSpec((tm, tk), lambda i, k: (i, k)),
                  pl.BlockSpec((tk, D_MODEL), lambda i, k: (k, 0)),
                  pl.BlockSpec((tm, D_MODEL), lambda i, k: (i, 0)),
                  pl.BlockSpec(ln_g.shape, lambda i, k: (0, 0)),
                  pl.BlockSpec(ln_b.shape, lambda i, k: (0, 0))],
        out_specs=pl.BlockSpec((tm, D_MODEL), lambda i, k: (i, 0)),
        out_shape=jax.ShapeDtypeStruct((rows, D_MODEL), F32),
        scratch_shapes=[pltpu.VMEM((tm, D_MODEL), F32)],
        compiler_params=_params("parallel", "arbitrary"),
        name="peer_out",
    )(a, v, x, ln_g, ln_b)


PEER_TB_SELECT = 256
PEER_TB_ACT = 256
PEER_TM_OUT, PEER_TK_OUT = 1024, 2048


def _peer(x, pw, ln_g, ln_b):
    rows = x.shape[0]
    i1, i2, gate = _peer_select(x, pw["wq"], pw["k1"], pw["k2"], min(PEER_TB_SELECT, rows))
    act = _peer_act(x, pw["u"], i1, i2, gate, min(PEER_TB_ACT, rows))
    return _peer_out(act, pw["v"], x, ln_g, ln_b, min(PEER_TM_OUT, rows), PEER_TK_OUT)


def _pad_cols(w, width):
    return jnp.pad(w, ((0, 0), (0, width - w.shape[1])))


def _row(v):
    return v.reshape(1, -1).astype(F32)


def _layer0_weights(w_in, mu, w0, w2, a0, a2, g2, kk, ka, rk, lnx_w, lnx_b, q_norm, w_uq, kv_norm,
                    w_uk, w_uv, w_out):
    c0 = 3 * A_W
    kpe0 = RWKV_COLS + Q_LORA + KV_LORA
    tile = lambda cols: jnp.tile(cols, (1, B_HEADS))
    w_in_p = jnp.concatenate([
        w_in[:, :c0],
        _pad_cols(w_in[:, c0:c0 + LORA_W], LANES),
        _pad_cols(w_in[:, c0 + LORA_W:c0 + LORA_W + LORA_A], LANES),
        w_in[:, c0 + LORA_W + LORA_A:RWKV_COLS],
        w_in[:, RWKV_COLS:kpe0],
        tile(w_in[:, kpe0:kpe0 + ROPE_HALF]),
        tile(w_in[:, kpe0 + ROPE_HALF:kpe0 + QK_ROPE]),
    ], axis=1).astype(BF16)
    mu_p = jnp.concatenate([mu[:c0], jnp.pad(mu[c0:c0 + LORA_W], (0, LANES - LORA_W)),
                            jnp.pad(mu[c0 + LORA_W:c0 + LORA_W + LORA_A], (0, LANES - LORA_A)),
                            mu[c0 + LORA_W + LORA_A:]])
    wq3 = w_uq.reshape(Q_LORA, B_HEADS, QK_NOPE + QK_ROPE)
    nope = jnp.pad(wq3[:, :, :QK_NOPE], ((0, 0), (0, 0), (0, LANES - QK_NOPE)))
    w_uq_p = jnp.concatenate([
        nope.reshape(Q_LORA, B_HEADS * LANES),
        wq3[:, :, QK_NOPE:QK_NOPE + ROPE_HALF].reshape(Q_LORA, LANES),
        wq3[:, :, QK_NOPE + ROPE_HALF:].reshape(Q_LORA, LANES),
    ], axis=1).astype(BF16)
    w_uk_t = jnp.pad(jnp.transpose(w_uk, (1, 2, 0)), ((0, 0), (0, LANES - QK_NOPE), (0, 0)))
    heads = np.arange(B_HEADS)
    blk = np.zeros((B_HEADS, KV_LORA, B_HEADS * V_HD), np.float32)
    hm = np.zeros((B_HEADS, B_HEADS * V_HD), np.float32)
    pem = np.zeros((B_HEADS, 2 * LANES), np.float32)
    for h in heads:
        blk[h, :, h * V_HD:(h + 1) * V_HD] = 1.0
        hm[h, h * V_HD:(h + 1) * V_HD] = 1.0
        pem[h, h * ROPE_HALF:(h + 1) * ROPE_HALF] = 1.0
        pem[h, LANES + h * ROPE_HALF:LANES + (h + 1) * ROPE_HALF] = 1.0
    wuv_all = w_uv.reshape(KV_LORA, B_HEADS * V_HD)
    seg = np.kron(np.eye(A_HEADS, dtype=np.float32), np.ones((A_HD, A_HD), np.float32))
    return {
        "w_in": w_in_p, "mu": _row(mu_p), "w0": _row(w0),
        "w2": jnp.pad(w2, ((0, LANES - LORA_W), (0, 0))).astype(BF16), "a0": _row(a0),
        "a2": jnp.pad(a2, ((0, LANES - LORA_A), (0, 0))).astype(BF16), "g2": g2.astype(BF16),
        "kk": _row(kk), "ka": _row(ka), "rk": _row(rk), "lnx_w": _row(lnx_w), "lnx_b": _row(lnx_b),
        "q_norm": _row(q_norm), "w_uq": w_uq_p, "kv_norm": _row(kv_norm),
        "w_uk": w_uk_t.astype(BF16),
        "wuv_pad": (wuv_all[None] * jnp.asarray(blk)).astype(BF16),
        "wuv_all": wuv_all.astype(BF16), "head_mask": jnp.asarray(hm),
        "pe_mask": jnp.asarray(pem), "seg": jnp.asarray(seg, dtype=BF16),
        "w_out": w_out.astype(BF16),
    }


def _layer1_weights(w_in, b_i, b_f, mh_norm, w_out):
    hk = C_HEADS * C_DK
    padh = lambda w: jnp.pad(w.reshape(D_MODEL, C_HEADS, C_DK),
                             ((0, 0), (0, 0), (0, HEAD_PAD - C_DK))).reshape(D_MODEL, OFF_MK)
    w_in_p = jnp.concatenate([
        padh(w_in[:, :hk]), padh(w_in[:, hk:2 * hk]), w_in[:, 2 * hk:2 * hk + 2 * MIX_O],
        _pad_cols(w_in[:, 2 * hk + 2 * MIX_O:], LANES),
    ], axis=1).astype(BF16)
    gate_bias = jnp.pad(jnp.concatenate([b_i, b_f]), (0, LANES - 2 * C_HEADS)).reshape(1, LANES)
    return {"w_in": w_in_p, "gate_bias": gate_bias.astype(F32), "mh_norm": _row(mh_norm),
            "w_out": w_out.astype(BF16)}


def _rope_tables(pos):
    inv = ROPE_BASE ** (-jnp.arange(0, QK_ROPE, 2, dtype=F32) / QK_ROPE)
    ang = pos.astype(F32)[:, None] * inv[None, :]
    return jnp.tile(jnp.cos(ang), (1, B_HEADS)), jnp.tile(jnp.sin(ang), (1, B_HEADS))


def _unpack_kpe(kpe_rot, bsz, t_len):
    return jnp.concatenate([kpe_rot[:, :ROPE_HALF], kpe_rot[:, LANES:LANES + ROPE_HALF]],
                           axis=-1).reshape(bsz, t_len, QK_ROPE)


def kernel(x_prompt, x_sample, cache_ckv, cache_kpe, page_table, state_shift, state_wkv, state_mlstm_c, state_mlstm_n, state_mlstm_m, w_in_e, mu_e, w0_e, w2_e, a0_e, a2_e, g2_e, kk_e, ka_e, rk_e, lnx_w_e, lnx_b_e, q_norm_e, w_uq_e, kv_norm_e, w_uk_e, w_uv_e, w_out_e, w_in_o, b_i_o, b_f_o, mh_norm_o, w_out_o, peer_wq, peer_k1, peer_k2, peer_u, peer_v, ln1_g, ln1_b, ln2_g, ln2_b):
    bp, tp, _ = x_prompt.shape
    bs, ts, _ = x_sample.shape
    past_len = page_table.shape[1] * PAGE_SIZE
    xp = x_prompt.reshape(bp * tp, D_MODEL)
    xs = x_sample.reshape(bs * ts, D_MODEL)
    rows_s = bs * ts
    tm_s = min(256, rows_s)
    tm_p = 256

    e = 0
    w0 = _layer0_weights(w_in_e[e], mu_e[e], w0_e[e], w2_e[e], a0_e[e], a2_e[e], g2_e[e], kk_e[e],
                         ka_e[e], rk_e[e], lnx_w_e[e], lnx_b_e[e], q_norm_e[e], w_uq_e[e],
                         kv_norm_e[e], w_uk_e[e], w_uv_e[e], w_out_e[e])
    g1, b1 = _row(ln1_g[0]), _row(ln1_b[0])
    cos_p, sin_p = _rope_tables(jnp.arange(tp))
    cos_s, sin_s = _rope_tables(past_len + jnp.arange(ts))
    cos_s = jnp.tile(cos_s, (tm_s // ts, 1))
    sin_s = jnp.tile(sin_s, (tm_s // ts, 1))

    (r, w, k, v, kk, kka, g, bonus, qcat, _, c_p, kpe_p, kcat) = _layer0_prep(
        xp, None, w0, cos_p, sin_p, bp, tp // tm_p, tm_p, tp)
    ch = lambda a: _to_chains(a, bp, tp)
    s0_p = jnp.zeros((A_HD, A_HD, bp * A_HEADS), F32)
    y, s_p = _wkv_scan(ch(r), ch(w), ch(k), ch(v), ch(kk), ch(kka), s0_p, 32)
    y = _from_chains(y, bp, tp)
    out_b = _mla_prompt(qcat, kcat, w0["wuv_pad"], bp, tp, 256)
    xp1 = _layer0_out(y, g, bonus, out_b, xp, w0, g1, b1, tm_p)
    wkv_p = jnp.transpose(s_p.reshape(A_HD, A_HD, bp, A_HEADS), (2, 3, 1, 0))

    start = _mm(state_shift[e], w0["w_in"][:, :RW_PAD], bs)
    start = jnp.repeat(start, ts, axis=0)
    (r, w, k, v, kk, kka, g, bonus, qcat, qpe, c_s, kpe_s, _) = _layer0_prep(
        xs, start, w0, cos_s, sin_s, rows_s // tm_s, 1, tm_s, ts)
    ch = lambda a: _to_chains(a, bs, ts)
    s0_s = jnp.transpose(state_wkv[e], (3, 2, 0, 1)).reshape(A_HD, A_HD, bs * A_HEADS)
    y, s_s = _wkv_scan(ch(r), ch(w), ch(k), ch(v), ch(kk), ch(kka), s0_s, ts)
    y = _from_chains(y, bs, ts)
    q_lat = qcat.reshape(bs, ts * B_HEADS, QCAT)[:, :, :KV_LORA]
    q_pe = jnp.transpose(qpe.reshape(bs, ts, 2, B_HEADS, ROPE_HALF), (0, 1, 3, 2, 4))
    q_pe = q_pe.reshape(bs, ts * B_HEADS, QK_ROPE).astype(BF16)
    kpe_new = _unpack_kpe(kpe_s, bs, ts)
    pad_new = lambda a: jnp.pad(a, ((0, 0), (0, SUBLANES - ts), (0, 0)))
    out_b = _mla_sample(q_lat, q_pe, pad_new(c_s.reshape(bs, ts, KV_LORA)), pad_new(kpe_new),
                        cache_ckv, cache_kpe, page_table, w0["wuv_all"], w0["head_mask"], e, ts)
    xs1 = _layer0_out(y, g, bonus, out_b.reshape(rows_s, B_HEADS * V_HD), xs, w0, g1, b1, tm_s)
    wkv_s = jnp.transpose(s_s.reshape(A_HD, A_HD, bs, A_HEADS), (2, 3, 1, 0))

    new_ckv_prompt = c_p.reshape(1, bp, tp, KV_LORA)
    new_kpe_prompt = _unpack_kpe(kpe_p, bp, tp)[None]
    new_ckv_sample = c_s.reshape(1, bs, ts, KV_LORA)
    new_kpe_sample = kpe_new[None]
    new_shift_prompt = x_prompt[:, -1][None]
    new_shift_sample = x_sample[:, -1][None]

    def peer_weights(l):
        return {"wq": peer_wq[l].astype(BF16), "k1": peer_k1[l].astype(BF16),
                "k2": peer_k2[l].astype(BF16), "u": peer_u[l].astype(BF16),
                "v": peer_v[l].astype(BF16)}

    pw = peer_weights(0)
    g2n, b2n = _row(ln2_g[0]), _row(ln2_b[0])
    xp2 = _peer(xp1, pw, g2n, b2n)
    xs2 = _peer(xs1, pw, g2n, b2n)

    od = 0
    w1 = _layer1_weights(w_in_o[od], b_i_o[od], b_f_o[od], mh_norm_o[od], w_out_o[od])
    g1, b1 = _row(ln1_g[1]), _row(ln1_b[1])
    pad_dk = lambda a: jnp.pad(a, [(0, 0)] * (a.ndim - 1) + [(0, HEAD_PAD - C_DK)])

    q, k, v, o_sig, gates = _layer1_prep(xp2, w1["w_in"], w1["gate_bias"], tm_p)
    zc = jnp.zeros((bp, C_HEADS, C_DV, HEAD_PAD), F32)
    zn = jnp.zeros((bp, C_HEADS, LANES), F32)
    h, c_p1, n_p1, m_p1 = _mlstm(q, k, v, gates, zc, zn, zn, bp, tp, MLSTM_CHUNK)
    xp3 = _layer1_out(h, o_sig, xp2, w1["mh_norm"], w1["w_out"], g1, b1, tm_p)

    q, k, v, o_sig, gates = _layer1_prep(xs2, w1["w_in"], w1["gate_bias"], tm_s)
    lp = MLSTM_CHUNK
    pad_t = lambda a: jnp.pad(a.reshape(bs, ts, -1), ((0, 0), (0, lp - ts), (0, 0))).reshape(
        bs * lp, -1)
    lane = jnp.arange(LANES)
    gate_fill = jnp.where(lane < C_HEADS, GATE_PAD, 0.0).astype(F32)
    gates_p = jnp.concatenate(
        [gates.reshape(bs, ts, LANES), jnp.broadcast_to(gate_fill, (bs, lp - ts, LANES))],
        axis=1).reshape(bs * lp, LANES)
    m0 = jnp.broadcast_to(state_mlstm_m[od][:, :, None], (bs, C_HEADS, LANES))
    h, c_s1, n_s1, m_s1 = _mlstm(pad_t(q), pad_t(k), pad_t(v), gates_p, pad_dk(state_mlstm_c[od]),
                                 pad_dk(state_mlstm_n[od]), m0, bs, lp, lp)
    h = h.reshape(bs, lp, MIX_O)[:, :ts].reshape(rows_s, MIX_O)
    xs3 = _layer1_out(h, o_sig, xs2, w1["mh_norm"], w1["w_out"], g1, b1, tm_s)

    pw = peer_weights(1)
    g2n, b2n = _row(ln2_g[1]), _row(ln2_b[1])
    xp4 = _peer(xp3, pw, g2n, b2n)
    xs4 = _peer(xs3, pw, g2n, b2n)

    return (xp4.reshape(bp, tp, D_MODEL), xs4.reshape(bs, ts, D_MODEL),
            new_ckv_prompt, new_kpe_prompt, new_ckv_sample, new_kpe_sample,
            new_shift_prompt, new_shift_sample, wkv_p[None], wkv_s[None],
            c_p1[..., :C_DK][None], c_s1[..., :C_DK][None],
            n_p1[..., :C_DK][None], n_s1[..., :C_DK][None],
            m_p1[..., 0][None], m_s1[..., 0][None])
```

```python
import functools

import jax
import jax.numpy as jnp
import numpy as np
from jax import lax
from jax.experimental import pallas as pl
from jax.experimental.pallas import tpu as pltpu

F32 = jnp.float32
BF16 = jnp.bfloat16
NEG_INF = float("-inf")

LANES = 128
SUBLANES = 8
VMEM_LIMIT_BYTES = 56 * 1024 * 1024

D_MODEL = 1024
DEPTH = 2
PAGE_SIZE = 128
ALPHA = (2 * DEPTH) ** 0.25
LN_EPS = 1e-5
A_HEADS, A_HD = 8, 64
A_W = A_HEADS * A_HD
LORA_W, LORA_A, LORA_G = 64, 64, 128
RWKV_COLS = 3 * A_W + LORA_W + LORA_A + LORA_G
GN_EPS = 64e-5
B_HEADS, QK_NOPE, QK_ROPE, V_HD = 8, 64, 32, 64
Q_LORA, KV_LORA = 384, 256
SM_SCALE = (QK_NOPE + QK_ROPE) ** -0.5
ROPE_BASE = 10000.0
ROPE_HALF = QK_ROPE // 2
C_HEADS, C_DK, C_DV = 8, 64, 128
MIX_O = C_HEADS * C_DV
MLSTM_CHUNK = 64
GATE_CAP = 15.0
GATE_PAD = -1e30
N_KEYS = 128
N_EXPERTS = N_KEYS * N_KEYS
PEER_HEADS, PEER_QDIM, PEER_TOPK = 8, 256, 16
PEER_HALF = PEER_QDIM // 2
PEER_PAIRS = PEER_HEADS * PEER_TOPK

RW_PAD = 3 * A_W + 3 * LANES
OFF_CQ = RW_PAD
OFF_CKV = OFF_CQ + Q_LORA
OFF_KP1 = OFF_CKV + KV_LORA
OFF_KP2 = OFF_KP1 + LANES
IN_E_PAD = OFF_KP2 + LANES
Q_PAD = B_HEADS * LANES + 2 * LANES
QCAT = 2 * KV_LORA
HEAD_PAD = LANES
OFF_MK = C_HEADS * HEAD_PAD
OFF_MV = 2 * C_HEADS * HEAD_PAD
OFF_MO = OFF_MV + MIX_O
OFF_MG = OFF_MO + MIX_O
IN_O_PAD = OFF_MG + LANES


def _params(*sem):
    return pltpu.CompilerParams(dimension_semantics=sem, vmem_limit_bytes=VMEM_LIMIT_BYTES)


def _dot(a, b):
    return jnp.dot(a, b, preferred_element_type=F32)


def _dot_nt(a, b):
    return lax.dot_general(a, b, (((1,), (1,)), ((), ())), preferred_element_type=F32)


def _dot_tn(a, b):
    return lax.dot_general(a, b, (((0,), (0,)), ((), ())), preferred_element_type=F32)


def _seg_sum(x, seg):
    hi = x.astype(BF16)
    lo = (x - hi.astype(F32)).astype(BF16)
    return _dot(hi, seg) + _dot(lo, seg)


def _layer_norm_res(x, h, g, b):
    z = ALPHA * x + h
    mu = jnp.mean(z, axis=-1, keepdims=True)
    zc = z - mu
    var = jnp.mean(zc * zc, axis=-1, keepdims=True)
    return zc * lax.rsqrt(var + LN_EPS) * g + b


def _full(shape):
    nd = len(shape)
    return pl.BlockSpec(shape, lambda *_: (0,) * nd)


def _mm_kernel(x_ref, w_ref, o_ref):
    o_ref[...] = _dot(x_ref[...].astype(BF16), w_ref[...])


def _mm(x, w, tm):
    m, k = x.shape
    n = w.shape[1]
    return pl.pallas_call(
        _mm_kernel,
        grid=(m // tm,),
        in_specs=[pl.BlockSpec((tm, k), lambda i: (i, 0)), _full((k, n))],
        out_specs=pl.BlockSpec((tm, n), lambda i: (i, 0)),
        out_shape=jax.ShapeDtypeStruct((m, n), F32),
        compiler_params=_params("parallel"),
        name="mm",
    )(x, w)


def _ka_kernel(has_start, seq_len, tm, *refs):
    if has_start:
        x_ref, start_ref = refs[:2]
        refs = refs[2:]
    else:
        x_ref = refs[0]
        start_ref = None
        refs = refs[1:]
    (w_in_ref, mu_ref, w0_ref, w2_ref, a0_ref, a2_ref, g2_ref, kk_ref, ka_ref, rk_ref, qn_ref,
     wuq_ref, kvn_ref, wuk_ref, cos_ref, sin_ref, seg_ref, pem_ref,
     r_out, w_out, k_out, v_out, kk_out, kka_out, g_out, bonus_out, qcat_out, qpe_out, c_out,
     kpe_out, kcat_out, carry_ref) = refs
    t_blk = pl.program_id(1)
    proj = _dot(x_ref[...].astype(BF16), w_in_ref[...])
    p_rw = proj[:, :RW_PAD]
    row = lax.broadcasted_iota(jnp.int32, (tm, 1), 0)
    prev = pltpu.roll(p_rw, 1, 0)
    if has_start:
        prev = jnp.where(row % seq_len == 0, start_ref[...], prev)
    else:
        @pl.when(t_blk == 0)
        def _():
            carry_ref[...] = jnp.zeros_like(carry_ref)

        prev = jnp.where(row == 0, carry_ref[SUBLANES - 1:SUBLANES, :], prev)
        carry_ref[...] = p_rw[tm - SUBLANES:, :]
    p = p_rw + (prev - p_rw) * mu_ref[...]
    r = p[:, 0:A_W]
    k = p[:, A_W:2 * A_W]
    v = p[:, 2 * A_W:3 * A_W]
    lw = p[:, 3 * A_W:3 * A_W + LANES]
    la = p[:, 3 * A_W + LANES:3 * A_W + 2 * LANES]
    lg = p[:, 3 * A_W + 2 * LANES:RW_PAD]
    w_pre = w0_ref[...] + _dot(jnp.tanh(lw).astype(BF16), w2_ref[...])
    w_log = -jax.nn.softplus(-w_pre) - 0.5
    decay = jnp.exp(-jnp.exp(w_log))
    a = jax.nn.sigmoid(a0_ref[...] + _dot(la.astype(BF16), a2_ref[...]))
    g = _dot(jax.nn.sigmoid(lg).astype(BF16), g2_ref[...])
    seg = seg_ref[...]
    kk = k * kk_ref[...]
    kk = kk / jnp.maximum(jnp.sqrt(_seg_sum(kk * kk, seg)), 1e-12)
    k = k * (1.0 + (a - 1.0) * ka_ref[...])
    r_out[...] = r
    w_out[...] = decay
    k_out[...] = k
    v_out[...] = v
    kk_out[...] = kk
    kka_out[...] = kk * a
    g_out[...] = g
    bonus_out[...] = _seg_sum(r * k * rk_ref[...], seg) * v
    cq = proj[:, OFF_CQ:OFF_CKV]
    ckv = proj[:, OFF_CKV:OFF_KP1]
    kp1 = proj[:, OFF_KP1:OFF_KP2]
    kp2 = proj[:, OFF_KP2:IN_E_PAD]
    cos = cos_ref[...]
    sin = sin_ref[...]
    cqn = cq * lax.rsqrt(jnp.mean(cq * cq, axis=-1, keepdims=True) + 1e-6) * qn_ref[...]
    q = _dot(cqn.astype(BF16), wuq_ref[...])
    x1 = q[:, B_HEADS * LANES:B_HEADS * LANES + LANES]
    x2 = q[:, B_HEADS * LANES + LANES:Q_PAD]
    qpe = jnp.concatenate([x1 * cos - x2 * sin, x1 * sin + x2 * cos], axis=-1)
    qpe_out[...] = qpe
    for h in range(B_HEADS):
        q_lat = _dot(q[:, h * LANES:(h + 1) * LANES].astype(BF16), wuk_ref[h])
        qcat_out[:, h * QCAT:h * QCAT + KV_LORA] = q_lat.astype(BF16)
        qcat_out[:, h * QCAT + KV_LORA:(h + 1) * QCAT] = (qpe * pem_ref[h:h + 1, :]).astype(BF16)
    c = ckv * lax.rsqrt(jnp.mean(ckv * ckv, axis=-1, keepdims=True) + 1e-6) * kvn_ref[...]
    kpe = jnp.concatenate([kp1 * cos - kp2 * sin, kp1 * sin + kp2 * cos], axis=-1)
    c_out[...] = c
    kpe_out[...] = kpe
    kcat_out[:, :KV_LORA] = c.astype(BF16)
    kcat_out[:, KV_LORA:] = kpe.astype(BF16)


def _layer0_prep(x, start, wts, cos, sin, n_seq_blocks, n_t_blocks, tm, seq_len):
    rows = x.shape[0]
    has_start = start is not None
    rmap = lambda b, t: (b * n_t_blocks + t, 0)
    tmap = lambda b, t: (t, 0)
    ins = [x] + ([start] if has_start else [])
    in_specs = [pl.BlockSpec((tm, D_MODEL), rmap)]
    if has_start:
        in_specs.append(pl.BlockSpec((tm, RW_PAD), rmap))
    w_names = ("w_in", "mu", "w0", "w2", "a0", "a2", "g2", "kk", "ka", "rk", "q_norm", "w_uq",
               "kv_norm", "w_uk")
    for nme in w_names:
        ins.append(wts[nme])
        in_specs.append(_full(wts[nme].shape))
    ins += [cos, sin, wts["seg"], wts["pe_mask"]]
    in_specs += [pl.BlockSpec((tm, LANES), tmap), pl.BlockSpec((tm, LANES), tmap),
                 _full(wts["seg"].shape), _full(wts["pe_mask"].shape)]
    widths = [A_W] * 8 + [B_HEADS * QCAT, 2 * LANES, KV_LORA, 2 * LANES, QCAT]
    dtypes = [F32] * 8 + [BF16, F32, F32, F32, BF16]
    out_shape = [jax.ShapeDtypeStruct((rows, wd), dt) for wd, dt in zip(widths, dtypes)]
    out_specs = [pl.BlockSpec((tm, wd), rmap) for wd in widths]
    return pl.pallas_call(
        functools.partial(_ka_kernel, has_start, seq_len, tm),
        grid=(n_seq_blocks, n_t_blocks),
        in_specs=in_specs,
        out_specs=out_specs,
        out_shape=out_shape,
        scratch_shapes=[pltpu.VMEM((SUBLANES, RW_PAD), F32)],
        compiler_params=_params("parallel", "arbitrary"),
        name="layer0_prep",
    )(*ins)


def _wkv_kernel(tch, r_ref, w_ref, k_ref, v_ref, kk_ref, kka_ref, s0_ref, y_ref, sout_ref,
                s_ref, sa_ref):
    tc = pl.program_id(1)

    @pl.when(tc == 0)
    def _():
        s_ref[...] = s0_ref[...]

    acc = jnp.zeros((A_HD, LANES), F32)
    for j in range(A_HD):
        acc = acc + s_ref[j] * kk_ref[0, pl.ds(j, 1), :]
    sa_ref[...] = -acc

    def step(t, carry):
        tn = jnp.minimum(t + 1, tch - 1)
        sa = sa_ref[...]
        v_t = v_ref[t]
        y = jnp.zeros((A_HD, LANES), F32)
        san = jnp.zeros((A_HD, LANES), F32)
        for j in range(A_HD):
            row = pl.ds(j, 1)
            sj = s_ref[j] * w_ref[t, row, :] + sa * kka_ref[t, row, :] + v_t * k_ref[t, row, :]
            s_ref[j] = sj
            y = y + sj * r_ref[t, row, :]
            san = san + sj * kk_ref[tn, row, :]
        y_ref[t] = y
        sa_ref[...] = -san
        return carry

    lax.fori_loop(0, tch, step, 0)

    @pl.when(tc == pl.num_programs(1) - 1)
    def _():
        sout_ref[...] = s_ref[...]


def _wkv_scan(r, w, k, v, kk, kka, s0, tch):
    t_len, _, n_chains = r.shape
    pad = (-n_chains) % LANES
    if pad:
        padc = lambda a: jnp.pad(a, ((0, 0), (0, 0), (0, pad)))
        y, s = _wkv_scan(*(padc(a) for a in (r, w, k, v, kk, kka, s0)), tch)
        return y[..., :n_chains], s[..., :n_chains]
    chains = n_chains
    seq = pl.BlockSpec((tch, A_HD, LANES), lambda c, t: (t, 0, c))
    st = pl.BlockSpec((A_HD, A_HD, LANES), lambda c, t: (0, 0, c))
    return pl.pallas_call(
        functools.partial(_wkv_kernel, tch),
        grid=(chains // LANES, t_len // tch),
        in_specs=[seq] * 6 + [st],
        out_specs=[seq, st],
        out_shape=[jax.ShapeDtypeStruct((t_len, A_HD, chains), F32),
                   jax.ShapeDtypeStruct((A_HD, A_HD, chains), F32)],
        scratch_shapes=[pltpu.VMEM((A_HD, A_HD, LANES), F32), pltpu.VMEM((A_HD, LANES), F32)],
        compiler_params=_params("parallel", "arbitrary"),
        name="wkv_scan",
    )(r, w, k, v, kk, kka, s0)


def _to_chains(x, bsz, t_len):
    x = x.reshape(bsz, t_len, A_HEADS, A_HD)
    return jnp.transpose(x, (1, 3, 0, 2)).reshape(t_len, A_HD, bsz * A_HEADS)


def _from_chains(y, bsz, t_len):
    y = y.reshape(t_len, A_HD, bsz, A_HEADS)
    return jnp.transpose(y, (2, 0, 3, 1)).reshape(bsz * t_len, A_W)


def _kb_kernel(y_ref, g_ref, bonus_ref, ob_ref, x_ref, lw_ref, lb_ref, seg_ref, wo_ref, ng_ref,
               nb_ref, o_ref):
    seg = seg_ref[...]
    y = y_ref[...]
    mu = _seg_sum(y, seg) * (1.0 / A_HD)
    yc = y - mu
    var = _seg_sum(yc * yc, seg) * (1.0 / A_HD)
    y = yc * lax.rsqrt(var + GN_EPS) * lw_ref[...] + lb_ref[...] + bonus_ref[...]
    out_a = (y * g_ref[...]).astype(BF16)
    h = _dot(out_a, wo_ref[:A_W, :]) + _dot(ob_ref[...].astype(BF16), wo_ref[A_W:, :])
    o_ref[...] = _layer_norm_res(x_ref[...], h, ng_ref[...], nb_ref[...])


def _layer0_out(y, g, bonus, out_b, x, wts, ln_g, ln_b, tm):
    rows = x.shape[0]
    half = pl.BlockSpec((tm, A_W), lambda i: (i, 0))
    full = pl.BlockSpec((tm, D_MODEL), lambda i: (i, 0))
    consts = [wts["lnx_w"], wts["lnx_b"], wts["seg"], wts["w_out"], ln_g, ln_b]
    return pl.pallas_call(
        _kb_kernel,
        grid=(rows // tm,),
        in_specs=[half, half, half, half, full] + [_full(c.shape) for c in consts],
        out_specs=full,
        out_shape=jax.ShapeDtypeStruct((rows, D_MODEL), F32),
        compiler_params=_params("parallel"),
        name="layer0_out",
    )(y, g, bonus, out_b, x, *consts)


def _lane_tiles(x):
    return [x[:, j * LANES:(j + 1) * LANES] for j in range(x.shape[1] // LANES)]


def _mla_p_kernel(tq, q_ref, kc_ref, wuv_ref, o_ref, m_ref, l_ref, acc_ref):
    qi = pl.program_id(1)
    rr = lax.broadcasted_iota(jnp.int32, (tq, tq), 0)
    cc = lax.broadcasted_iota(jnp.int32, (tq, tq), 1)
    causal = cc <= rr
    m_ref[...] = jnp.full(m_ref.shape, NEG_INF, F32)
    l_ref[...] = jnp.zeros(l_ref.shape, F32)
    acc_ref[...] = jnp.zeros(acc_ref.shape, F32)

    def chunk(c, mask):
        kc = kc_ref[pl.ds(pl.multiple_of(c * tq, tq), tq), :]

        def scores(h):
            return _dot_nt(q_ref[:, h * QCAT:(h + 1) * QCAT], kc)

        lane_tiles = _lane_tiles
        s_next = scores(0)
        for h in range(B_HEADS):
            s = s_next * SM_SCALE
            if h + 1 < B_HEADS:
                s_next = scores(h + 1)
            if mask:
                s = jnp.where(causal, s, NEG_INF)
            m = m_ref[h]
            s_tiles = lane_tiles(s)
            s_max = functools.reduce(jnp.maximum, s_tiles)
            m_new = jnp.maximum(m, jnp.max(s_max, axis=-1, keepdims=True))
            alpha = jnp.exp(m - m_new)
            p_tiles = [jnp.exp(st - m_new) for st in s_tiles]
            p_sum = functools.reduce(jnp.add, p_tiles)
            m_ref[h] = m_new
            l_ref[h] = alpha * l_ref[h] + jnp.sum(p_sum, axis=-1, keepdims=True)
            p = jnp.concatenate(p_tiles, axis=-1).astype(BF16)
            pv = _dot(p, kc[:, :KV_LORA])
            acc_ref[h] = jnp.concatenate(
                [alpha * a + o for a, o in zip(lane_tiles(acc_ref[h]), lane_tiles(pv))], axis=-1)

    def body(c, carry):
        chunk(c, False)
        return carry

    lax.fori_loop(0, qi, body, 0)
    chunk(qi, True)
    out = jnp.zeros((tq, B_HEADS * V_HD), F32)
    for h in range(B_HEADS):
        l = l_ref[h]
        o_lat = jnp.concatenate([a / l for a in _lane_tiles(acc_ref[h])], axis=-1)
        out = out + _dot(o_lat.astype(BF16), wuv_ref[h])
    o_ref[...] = out


def _mla_prompt(qcat, kcat, wuv_pad, bsz, t_len, tq):
    nq = t_len // tq
    return pl.pallas_call(
        functools.partial(_mla_p_kernel, tq),
        grid=(bsz, nq),
        in_specs=[pl.BlockSpec((tq, B_HEADS * QCAT), lambda b, i: (b * nq + i, 0)),
                  pl.BlockSpec((t_len, QCAT), lambda b, i: (b, 0)),
                  _full(wuv_pad.shape)],
        out_specs=pl.BlockSpec((tq, B_HEADS * V_HD), lambda b, i: (b * nq + i, 0)),
        out_shape=jax.ShapeDtypeStruct((bsz * t_len, B_HEADS * V_HD), F32),
        scratch_shapes=[pltpu.VMEM((B_HEADS, tq, LANES), F32),
                        pltpu.VMEM((B_HEADS, tq, LANES), F32),
                        pltpu.VMEM((B_HEADS, tq, KV_LORA), F32)],
        compiler_params=_params("parallel", "arbitrary"),
        name="mla_prompt",
    )(qcat, kcat, wuv_pad)


PAGES_PER_STEP = 64


def _mla_s_kernel(n_new, pt_ref, ql_ref, qp_ref, cn_ref, kn_ref, *rest):
    ck_refs = rest[:PAGES_PER_STEP]
    kp_refs = rest[PAGES_PER_STEP:2 * PAGES_PER_STEP]
    wuv_ref, hm_ref, o_ref, m_ref, l_ref, acc_ref = rest[2 * PAGES_PER_STEP:]
    step = pl.program_id(1)
    ql = ql_ref[...]
    qp = qp_ref[...]
    n_rows = ql.shape[0]

    @pl.when(step == 0)
    def _():
        cn = cn_ref[...].astype(BF16)
        kn = kn_ref[...].astype(BF16)
        s = (_dot_nt(ql, cn) + _dot_nt(qp, kn)) * SM_SCALE
        tok = lax.broadcasted_iota(jnp.int32, s.shape, 0) // B_HEADS
        col = lax.broadcasted_iota(jnp.int32, s.shape, 1)
        s = jnp.where(col <= tok, s, NEG_INF)
        m = jnp.max(s, axis=-1, keepdims=True)
        p = jnp.exp(s - m)
        m_ref[...] = m
        l_ref[...] = jnp.sum(p, axis=-1, keepdims=True)
        acc_ref[...] = _dot(p.astype(BF16), cn)

    cb = jnp.concatenate([r[...].astype(BF16) for r in ck_refs], axis=0)
    kb = jnp.concatenate([r[...].astype(BF16) for r in kp_refs], axis=1)
    s = (_dot_nt(ql, cb) + _dot(qp, kb)) * SM_SCALE
    m = m_ref[...]
    m_new = jnp.maximum(m, jnp.max(s, axis=-1, keepdims=True))
    alpha = jnp.exp(m - m_new)
    p = jnp.exp(s - m_new)
    m_ref[...] = m_new
    l_ref[...] = alpha * l_ref[...] + jnp.sum(p, axis=-1, keepdims=True)
    acc_ref[...] = alpha * acc_ref[...] + _dot(p.astype(BF16), cb)

    @pl.when(step == pl.num_programs(1) - 1)
    def _():
        o = (acc_ref[...] / l_ref[...]).astype(BF16)
        full = _dot(o, wuv_ref[...])
        full = full.reshape(n_new, B_HEADS, B_HEADS * V_HD) * hm_ref[...]
        o_ref[...] = jnp.sum(full, axis=1)


def _mla_sample(q_lat, q_pe, c_new, k_new, cache_ckv, cache_kpe, page_table, wuv_all, head_mask,
                layer, n_new):
    n_seq, n_pages = page_table.shape
    n_steps = n_pages // PAGES_PER_STEP
    n_rows = q_lat.shape[1]

    def page_spec(shape, k):
        return pl.BlockSpec((None, None) + shape,
                            lambda b, s, pt: (layer, pt[b, s * PAGES_PER_STEP + k], 0, 0))

    in_specs = [pl.BlockSpec((None, n_rows, KV_LORA), lambda b, s, pt: (b, 0, 0)),
                pl.BlockSpec((None, n_rows, QK_ROPE), lambda b, s, pt: (b, 0, 0)),
                pl.BlockSpec((None, SUBLANES, KV_LORA), lambda b, s, pt: (b, 0, 0)),
                pl.BlockSpec((None, SUBLANES, QK_ROPE), lambda b, s, pt: (b, 0, 0))]
    in_specs += [page_spec((PAGE_SIZE, KV_LORA), k) for k in range(PAGES_PER_STEP)]
    in_specs += [page_spec((QK_ROPE, PAGE_SIZE), k) for k in range(PAGES_PER_STEP)]
    in_specs += [pl.BlockSpec(wuv_all.shape, lambda b, s, pt: (0, 0)),
                 pl.BlockSpec(head_mask.shape, lambda b, s, pt: (0, 0))]
    grid_spec = pltpu.PrefetchScalarGridSpec(
        num_scalar_prefetch=1,
        grid=(n_seq, n_steps),
        in_specs=in_specs,
        out_specs=pl.BlockSpec((None, n_new, B_HEADS * V_HD), lambda b, s, pt: (b, 0, 0)),
        scratch_shapes=[pltpu.VMEM((n_rows, 1), F32), pltpu.VMEM((n_rows, 1), F32),
                        pltpu.VMEM((n_rows, KV_LORA), F32)],
    )
    return pl.pallas_call(
        functools.partial(_mla_s_kernel, n_new),
        grid_spec=grid_spec,
        out_shape=jax.ShapeDtypeStruct((n_seq, n_new, B_HEADS * V_HD), F32),
        compiler_params=_params("parallel", "arbitrary"),
        name="mla_sample",
    )(page_table, q_lat, q_pe, c_new, k_new, *([cache_ckv] * PAGES_PER_STEP),
      *([jnp.swapaxes(cache_kpe, 2, 3)] * PAGES_PER_STEP), wuv_all, head_mask)


def _kc_kernel(x_ref, w_ref, gb_ref, q_out, k_out, v_out, o_out, g_out):
    proj = _dot(x_ref[...].astype(BF16), w_ref[...])
    q_out[...] = (proj[:, :OFF_MK] * (C_DK ** -0.5)).astype(BF16)
    k_out[...] = proj[:, OFF_MK:OFF_MV].astype(BF16)
    v_out[...] = proj[:, OFF_MV:OFF_MO].astype(BF16)
    o_out[...] = jax.nn.sigmoid(proj[:, OFF_MO:OFF_MG])
    pre = proj[:, OFF_MG:] + gb_ref[...]
    cap = GATE_CAP * jnp.tanh(pre / GATE_CAP)
    lane = lax.broadcasted_iota(jnp.int32, cap.shape, 1)
    g_out[...] = jnp.where(lane < C_HEADS, cap, jax.nn.log_sigmoid(cap))


def _layer1_prep(x, w_in, gate_bias, tm):
    rows = x.shape[0]
    widths = [OFF_MK, OFF_MK, MIX_O, MIX_O, LANES]
    dtypes = [BF16, BF16, BF16, F32, F32]
    return pl.pallas_call(
        _kc_kernel,
        grid=(rows // tm,),
        in_specs=[pl.BlockSpec((tm, D_MODEL), lambda i: (i, 0)), _full(w_in.shape),
                  _full(gate_bias.shape)],
        out_specs=[pl.BlockSpec((tm, wd), lambda i: (i, 0)) for wd in widths],
        out_shape=[jax.ShapeDtypeStruct((rows, wd), dt) for wd, dt in zip(widths, dtypes)],
        compiler_params=_params("parallel"),
        name="layer1_prep",
    )(x, w_in, gate_bias)


def _mlstm_kernel(chunk, q_ref, k_ref, v_ref, g_ref, c0_ref, n0_ref, m0_ref, h_ref, co_ref,
                  no_ref, mo_ref, c_s, n_s, m_s):
    ci = pl.program_id(1)

    @pl.when(ci == 0)
    def _():
        c_s[...] = c0_ref[...]
        n_s[...] = n0_ref[...]
        m_s[...] = m0_ref[...]

    g = g_ref[...]
    row = lax.broadcasted_iota(jnp.int32, g.shape, 0)
    lane = lax.broadcasted_iota(jnp.int32, g.shape, 1)
    b = g
    shift = 1
    while shift < chunk:
        b = b + jnp.where(row >= shift, pltpu.roll(b, shift, 0), 0.0)
        shift *= 2
    g_t = g.T
    b_t = b.T
    rr = lax.broadcasted_iota(jnp.int32, (chunk, chunk), 0)
    cc = lax.broadcasted_iota(jnp.int32, (chunk, chunk), 1)
    causal = cc <= rr
    heads = range(C_HEADS)
    qs = [q_ref[:, h * HEAD_PAD:(h + 1) * HEAD_PAD] for h in heads]
    ks = [k_ref[:, h * HEAD_PAD:(h + 1) * HEAD_PAD] for h in heads]
    qk_all = [_dot_nt(qs[h], ks[h]) for h in heads]
    qc_all = [_dot_nt(qs[h], c_s[h].astype(BF16)) for h in heads]
    for h in heads:
        b_col = jnp.sum(jnp.where(lane == C_HEADS + h, b, 0.0), axis=-1, keepdims=True)
        i_col = jnp.sum(jnp.where(lane == h, g, 0.0), axis=-1, keepdims=True)
        b_row = b_t[C_HEADS + h:C_HEADS + h + 1, :]
        i_row = g_t[h:h + 1, :]
        m_prev = m_s[h:h + 1, 0:1]
        d = jnp.where(causal, b_col - b_row + i_row, NEG_INF)
        gg = b_col + m_prev
        mt = jnp.maximum(gg, jnp.max(d, axis=-1, keepdims=True))
        w_inter = jnp.exp(gg - mt)
        qh, kh = qs[h], ks[h]
        vh = v_ref[:, h * C_DV:(h + 1) * C_DV]
        c_prev = c_s[h]
        n_prev = n_s[h:h + 1, :]
        att = jnp.exp(d - mt) * qk_all[h]
        num = w_inter * qc_all[h] + _dot(att.astype(BF16), vh)
        qn = jnp.sum(qh.astype(F32) * n_prev, axis=-1, keepdims=True)
        den = w_inter * qn + jnp.sum(att, axis=-1, keepdims=True)
        h_ref[:, h * C_DV:(h + 1) * C_DV] = num / jnp.maximum(jnp.abs(den), jnp.exp(-mt))
        m_new = mt[chunk - 1:chunk, :]
        b_last = b_col[chunk - 1:chunk, :]
        w_state = jnp.exp(b_last - b_col + i_col - m_new)
        dec = jnp.exp(b_last + m_prev - m_new)
        wv = (w_state * vh.astype(F32)).astype(BF16)
        c_s[h] = dec * c_prev + _dot_tn(wv, kh)
        n_s[h:h + 1, :] = dec * n_prev + jnp.sum(w_state * kh.astype(F32), axis=0, keepdims=True)
        m_s[h:h + 1, :] = jnp.broadcast_to(m_new, (1, LANES))

    @pl.when(ci == pl.num_programs(1) - 1)
    def _():
        co_ref[...] = c_s[...]
        no_ref[...] = n_s[...]
        mo_ref[...] = m_s[...]


def _mlstm(q, k, v, gates, c0, n0, m0, bsz, t_len, chunk):
    nc = t_len // chunk
    seq = lambda wd: pl.BlockSpec((chunk, wd), lambda b, c: (b * nc + c, 0))
    c_spec = pl.BlockSpec((None, C_HEADS, C_DV, HEAD_PAD), lambda b, c: (b, 0, 0, 0))
    v_spec = pl.BlockSpec((None, C_HEADS, LANES), lambda b, c: (b, 0, 0))
    return pl.pallas_call(
        functools.partial(_mlstm_kernel, chunk),
        grid=(bsz, nc),
        in_specs=[seq(OFF_MK), seq(OFF_MK), seq(MIX_O), seq(LANES), c_spec, v_spec, v_spec],
        out_specs=[seq(MIX_O), c_spec, v_spec, v_spec],
        out_shape=[jax.ShapeDtypeStruct((bsz * t_len, MIX_O), F32),
                   jax.ShapeDtypeStruct(c0.shape, F32),
                   jax.ShapeDtypeStruct(n0.shape, F32),
                   jax.ShapeDtypeStruct(m0.shape, F32)],
        scratch_shapes=[pltpu.VMEM((C_HEADS, C_DV, HEAD_PAD), F32),
                        pltpu.VMEM((C_HEADS, LANES), F32), pltpu.VMEM((C_HEADS, LANES), F32)],
        compiler_params=_params("parallel", "arbitrary"),
        name="mlstm",
    )(q, k, v, gates, c0, n0, m0)


def _kd_kernel(h_ref, o_ref, x_ref, mh_ref, wo_ref, ng_ref, nb_ref, out_ref):
    parts = []
    for h in range(C_HEADS):
        hh = h_ref[:, h * C_DV:(h + 1) * C_DV]
        parts.append(hh * lax.rsqrt(jnp.mean(hh * hh, axis=-1, keepdims=True) + 1e-6))
    hn = jnp.concatenate(parts, axis=-1) * mh_ref[...]
    y = _dot((hn * o_ref[...]).astype(BF16), wo_ref[...])
    out_ref[...] = _layer_norm_res(x_ref[...], y, ng_ref[...], nb_ref[...])


def _layer1_out(h, o_sig, x, mh_norm, w_out, ln_g, ln_b, tm):
    rows = x.shape[0]
    full = pl.BlockSpec((tm, D_MODEL), lambda i: (i, 0))
    consts = [mh_norm, w_out, ln_g, ln_b]
    return pl.pallas_call(
        _kd_kernel,
        grid=(rows // tm,),
        in_specs=[full, full, full] + [_full(c.shape) for c in consts],
        out_specs=full,
        out_shape=jax.ShapeDtypeStruct((rows, D_MODEL), F32),
        compiler_params=_params("parallel"),
        name="layer1_out",
    )(h, o_sig, x, *consts)


def _top16(s_ref, val_ref, idx_ref):
    n = s_ref.shape[0]
    row = lax.broadcasted_iota(jnp.int32, s_ref.shape, 0)
    s = s_ref[...]
    for r in range(PEER_TOPK):
        m = jnp.max(s, axis=0, keepdims=True)
        am = jnp.min(jnp.where(s == m, row, n), axis=0, keepdims=True)
        val_ref[r:r + 1, :] = m
        idx_ref[r:r + 1, :] = am
        s = jnp.where(row == am, NEG_INF, s)


def _sort16_network():
    n, pairs, p = PEER_TOPK, [], 1
    while p < n:
        k = p
        while k >= 1:
            for j in range(k % p, n - k, 2 * k):
                for i in range(min(k, n - j - k)):
                    if (i + j) // (2 * p) == (i + j + k) // (2 * p):
                        pairs.append((i + j, i + j + k))
            k //= 2
        p *= 2
    return pairs


SORT16 = _sort16_network()


def _top16_keys(s_ref, val_ref, idx_ref):
    n_slab = N_KEYS // SUBLANES
    row = lax.broadcasted_iota(jnp.int32, (SUBLANES, LANES), 0)
    for g in range(s_ref.shape[1] // LANES):
        cols = slice(g * LANES, (g + 1) * LANES)
        vals = [s_ref[v * SUBLANES:(v + 1) * SUBLANES, cols] for v in range(n_slab)]
        keys = [row + v * SUBLANES for v in range(n_slab)]
        for a, b in SORT16:
            keep = (vals[a] > vals[b]) | ((vals[a] == vals[b]) & (keys[a] < keys[b]))
            vals[a], vals[b] = (jnp.where(keep, vals[a], vals[b]),
                                jnp.where(keep, vals[b], vals[a]))
            keys[a], keys[b] = (jnp.where(keep, keys[a], keys[b]),
                                jnp.where(keep, keys[b], keys[a]))
        for r in range(PEER_TOPK):
            m = jnp.max(vals[0], axis=0, keepdims=True)
            am = jnp.min(jnp.where(vals[0] == m, keys[0], N_KEYS), axis=0, keepdims=True)
            val_ref[r:r + 1, cols] = m
            idx_ref[r:r + 1, cols] = am
            taken = keys[0] == am
            for v in range(PEER_TOPK - 1 - r):
                vals[v] = jnp.where(taken, vals[v + 1], vals[v])
                keys[v] = jnp.where(taken, keys[v + 1], keys[v])


def _pick_rank(src_ref, rank):
    out = jnp.zeros(rank.shape, jnp.int32)
    for r in range(PEER_TOPK):
        out = jnp.where(rank == r, src_ref[r:r + 1, :], out)
    return out


CAND_COUNT = [PEER_TOPK // (i + 1) for i in range(PEER_TOPK)]
CAND_START = [sum(CAND_COUNT[:i]) for i in range(PEER_TOPK + 1)]
CAND_ROWS = -(-CAND_START[PEER_TOPK] // SUBLANES) * SUBLANES


def _peer_q_kernel(tb, x_ref, wq_ref, k1_ref, k2_ref, i1_out, i2_out, g_out,
                   s_ref, cand_ref, t1_ref, j1_ref, t2_ref, j2_ref, tc_ref, jc_ref,
                   e1_ref, e2_ref, gt_ref):
    q = _dot(x_ref[...].astype(BF16), wq_ref[...])
    for h in range(PEER_HEADS):
        q1 = q[:, h * PEER_QDIM:h * PEER_QDIM + PEER_HALF].astype(BF16)
        q2 = q[:, h * PEER_QDIM + PEER_HALF:(h + 1) * PEER_QDIM].astype(BF16)
        s_ref[...] = _dot_nt(k1_ref[...], q1)
        _top16_keys(s_ref, t1_ref, j1_ref)
        s_ref[...] = _dot_nt(k2_ref[...], q2)
        _top16_keys(s_ref, t2_ref, j2_ref)
        for i in range(PEER_TOPK):
            lo, n = CAND_START[i], CAND_COUNT[i]
            cand_ref[lo:lo + n, :] = t1_ref[i:i + 1, :] + t2_ref[0:n, :]
        cand_ref[CAND_START[PEER_TOPK]:, :] = jnp.full(
            (CAND_ROWS - CAND_START[PEER_TOPK], tb), NEG_INF, F32)
        _top16(cand_ref, tc_ref, jc_ref)
        pos = jc_ref[...]
        rank1 = jnp.zeros_like(pos)
        start = jnp.zeros_like(pos)
        for i in range(1, PEER_TOPK):
            past = pos >= CAND_START[i]
            rank1 = rank1 + past.astype(jnp.int32)
            start = jnp.where(past, CAND_START[i], start)
        rows = slice(h * PEER_TOPK, (h + 1) * PEER_TOPK)
        e1_ref[rows, :] = _pick_rank(j1_ref, rank1).astype(F32)
        e2_ref[rows, :] = _pick_rank(j2_ref, pos - start).astype(F32)
        top = tc_ref[...]
        e = jnp.exp(top - top[0:1, :])
        gt_ref[rows, :] = e / jnp.sum(e, axis=0, keepdims=True)
    i1_out[...] = e1_ref[...].T.astype(jnp.int32)
    i2_out[...] = e2_ref[...].T.astype(jnp.int32)
    g_out[...] = gt_ref[...].T


def _peer_select(x, wq, k1, k2, tb):
    rows = x.shape[0]
    out = pl.BlockSpec((tb, PEER_PAIRS), lambda i: (i, 0))
    small = lambda dt: pltpu.VMEM((PEER_TOPK, tb), dt)
    return pl.pallas_call(
        functools.partial(_peer_q_kernel, tb),
        grid=(rows // tb,),
        in_specs=[pl.BlockSpec((tb, D_MODEL), lambda i: (i, 0)), _full(wq.shape), _full(k1.shape),
                  _full(k2.shape)],
        out_specs=[out, out, out],
        out_shape=[jax.ShapeDtypeStruct((rows, PEER_PAIRS), jnp.int32),
                   jax.ShapeDtypeStruct((rows, PEER_PAIRS), jnp.int32),
                   jax.ShapeDtypeStruct((rows, PEER_PAIRS), F32)],
        scratch_shapes=[pltpu.VMEM((N_KEYS, tb), F32),
                        pltpu.VMEM((CAND_ROWS, tb), F32),
                        small(F32), small(jnp.int32), small(F32), small(jnp.int32),
                        small(F32), small(jnp.int32),
                        pltpu.VMEM((PEER_PAIRS, tb), F32), pltpu.VMEM((PEER_PAIRS, tb), F32),
                        pltpu.VMEM((PEER_PAIRS, tb), F32)],
        compiler_params=_params("parallel"),
        name="peer_select",
    )(x, wq, k1, k2)


PEER_SPLIT = 2
PAIR_UNROLL = 16


def _peer_u_kernel(tb, n_i1, x_ref, u_ref, i1_ref, i2_ref, g_ref, a_ref, hs_ref, gs_ref, xb_ref):
    part = pl.program_id(0)
    n_pairs = n_i1 // 2
    tokens_per_pair = tb // n_pairs
    xb_ref[...] = x_ref[...].astype(BF16)
    sub1 = lax.broadcasted_iota(jnp.int32, (n_i1, PEER_PAIRS), 0) + part * n_i1
    sub2 = lax.broadcasted_iota(jnp.int32, (N_KEYS, PEER_PAIRS), 0)

    def slab_pair(pp):
        u2 = u_ref[pl.ds(pl.multiple_of(pp * 2 * N_KEYS, 2 * N_KEYS), 2 * N_KEYS), :]
        h2 = _dot_nt(xb_ref[...], u2)
        act = 0.5 * h2 * (1.0 + lax.erf(h2 * (2.0 ** -0.5)))
        base = pl.multiple_of(pp * 2 * tb, 2 * tb)
        hs_ref[pl.ds(base, tb), :] = act[:, :N_KEYS]
        hs_ref[pl.ds(base + tb, tb), :] = act[:, N_KEYS:]

    def gate_rows(t):
        i1 = i1_ref[pl.ds(t, 1), :]
        i2 = i2_ref[pl.ds(t, 1), :]
        gate = g_ref[pl.ds(t, 1), :]
        sel1 = jnp.where(sub1 == i1, gate, 0.0).astype(BF16)
        sel2 = jnp.where(sub2 == i2, 1.0, 0.0).astype(BF16)
        gs_ref[pl.ds(t, n_i1, stride=tb), :] = _dot_nt(sel1, sel2)

    def pair(pp, carry):
        slab_pair(pp)
        for k in range(tokens_per_pair):
            gate_rows(pp * tokens_per_pair + k)
        return carry

    lax.fori_loop(0, n_pairs, pair, 0, unroll=PAIR_UNROLL)
    for i in range(n_i1):
        rows = slice(i * tb, (i + 1) * tb)
        a_ref[:, i * N_KEYS:(i + 1) * N_KEYS] = (hs_ref[rows, :] * gs_ref[rows, :]).astype(BF16)


def _peer_act(x, u, i1, i2, gate, tb):
    rows = x.shape[0]
    n_i1 = N_KEYS // PEER_SPLIT
    n_exp = N_EXPERTS // PEER_SPLIT
    assert tb % (n_i1 // 2) == 0
    tok = lambda wd: pl.BlockSpec((tb, wd), lambda p, i: (i, 0))
    slab = pltpu.VMEM((n_i1 * tb, N_KEYS), F32)
    return pl.pallas_call(
        functools.partial(_peer_u_kernel, tb, n_i1),
        grid=(PEER_SPLIT, rows // tb),
        in_specs=[tok(D_MODEL),
                  pl.BlockSpec((n_exp, D_MODEL), lambda p, i: (p, 0),
                               pipeline_mode=pl.Buffered(1)),
                  tok(PEER_PAIRS), tok(PEER_PAIRS), tok(PEER_PAIRS)],
        out_specs=pl.BlockSpec((tb, n_exp), lambda p, i: (i, p)),
        out_shape=jax.ShapeDtypeStruct((rows, N_EXPERTS), BF16),
        scratch_shapes=[slab, slab, pltpu.VMEM((tb, D_MODEL), BF16)],
        compiler_params=_params("arbitrary", "arbitrary"),
        name="peer_act",
    )(x, u, i1, i2, gate)


def _peer_v_kernel(a_ref, v_ref, x_ref, ng_ref, nb_ref, o_ref, acc_ref):
    kk = pl.program_id(1)

    @pl.when(kk == 0)
    def _():
        acc_ref[...] = jnp.zeros_like(acc_ref)

    acc_ref[...] += _dot(a_ref[...], v_ref[...])

    @pl.when(kk == pl.num_programs(1) - 1)
    def _():
        o_ref[...] = _layer_norm_res(x_ref[...], acc_ref[...], ng_ref[...], nb_ref[...])


def _peer_out(a, v, x, ln_g, ln_b, tm, tk):
    rows = x.shape[0]
    return pl.pallas_call(
        _peer_v_kernel,
        grid=(rows // tm, N_EXPERTS // tk),
        in_specs=[pl.BlockSpec((tm, tk), lambda i, k: (i, k)),
                  pl.BlockSpec((tk, D_MODEL), lambda i, k: (k, 0)),
                  pl.BlockSpec((tm, D_MODEL), lambda i, k: (i, 0)),
                  pl.BlockSpec(ln_g.shape, lambda i, k: (0, 0)),
                  pl.BlockSpec(ln_b.shape, lambda i, k: (0, 0))],
        out_specs=pl.BlockSpec((tm, D_MODEL), lambda i, k: (i, 0)),
        out_shape=jax.ShapeDtypeStruct((rows, D_MODEL), F32),
        scratch_shapes=[pltpu.VMEM((tm, D_MODEL), F32)],
        compiler_params=_params("parallel", "arbitrary"),
        name="peer_out",
    )(a, v, x, ln_g, ln_b)


PEER_TB_SELECT = 256
PEER_TB_ACT = 256
PEER_TM_OUT, PEER_TK_OUT = 1024, 2048


def _peer(x, pw, ln_g, ln_b):
    rows = x.shape[0]
    i1, i2, gate = _peer_select(x, pw["wq"], pw["k1"], pw["k2"], min(PEER_TB_SELECT, rows))
    act = _peer_act(x, pw["u"], i1, i2, gate, min(PEER_TB_ACT, rows))
    return _peer_out(act, pw["v"], x, ln_g, ln_b, min(PEER_TM_OUT, rows), PEER_TK_OUT)


def _pad_cols(w, width):
    return jnp.pad(w, ((0, 0), (0, width - w.shape[1])))


def _row(v):
    return v.reshape(1, -1).astype(F32)


def _layer0_weights(w_in, mu, w0, w2, a0, a2, g2, kk, ka, rk, lnx_w, lnx_b, q_norm, w_uq, kv_norm,
                    w_uk, w_uv, w_out):
    c0 = 3 * A_W
    kpe0 = RWKV_COLS + Q_LORA + KV_LORA
    tile = lambda cols: jnp.tile(cols, (1, B_HEADS))
    w_in_p = jnp.concatenate([
        w_in[:, :c0],
        _pad_cols(w_in[:, c0:c0 + LORA_W], LANES),
        _pad_cols(w_in[:, c0 + LORA_W:c0 + LORA_W + LORA_A], LANES),
        w_in[:, c0 + LORA_W + LORA_A:RWKV_COLS],
        w_in[:, RWKV_COLS:kpe0],
        tile(w_in[:, kpe0:kpe0 + ROPE_HALF]),
        tile(w_in[:, kpe0 + ROPE_HALF:kpe0 + QK_ROPE]),
    ], axis=1).astype(BF16)
    mu_p = jnp.concatenate([mu[:c0], jnp.pad(mu[c0:c0 + LORA_W], (0, LANES - LORA_W)),
                            jnp.pad(mu[c0 + LORA_W:c0 + LORA_W + LORA_A], (0, LANES - LORA_A)),
                            mu[c0 + LORA_W + LORA_A:]])
    wq3 = w_uq.reshape(Q_LORA, B_HEADS, QK_NOPE + QK_ROPE)
    nope = jnp.pad(wq3[:, :, :QK_NOPE], ((0, 0), (0, 0), (0, LANES - QK_NOPE)))
    w_uq_p = jnp.concatenate([
        nope.reshape(Q_LORA, B_HEADS * LANES),
        wq3[:, :, QK_NOPE:QK_NOPE + ROPE_HALF].reshape(Q_LORA, LANES),
        wq3[:, :, QK_NOPE + ROPE_HALF:].reshape(Q_LORA, LANES),
    ], axis=1).astype(BF16)
    w_uk_t = jnp.pad(jnp.transpose(w_uk, (1, 2, 0)), ((0, 0), (0, LANES - QK_NOPE), (0, 0)))
    heads = np.arange(B_HEADS)
    blk = np.zeros((B_HEADS, KV_LORA, B_HEADS * V_HD), np.float32)
    hm = np.zeros((B_HEADS, B_HEADS * V_HD), np.float32)
    pem = np.zeros((B_HEADS, 2 * LANES), np.float32)
    for h in heads:
        blk[h, :, h * V_HD:(h + 1) * V_HD] = 1.0
        hm[h, h * V_HD:(h + 1) * V_HD] = 1.0
        pem[h, h * ROPE_HALF:(h + 1) * ROPE_HALF] = 1.0
        pem[h, LANES + h * ROPE_HALF:LANES + (h + 1) * ROPE_HALF] = 1.0
    wuv_all = w_uv.reshape(KV_LORA, B_HEADS * V_HD)
    seg = np.kron(np.eye(A_HEADS, dtype=np.float32), np.ones((A_HD, A_HD), np.float32))
    return {
        "w_in": w_in_p, "mu": _row(mu_p), "w0": _row(w0),
        "w2": jnp.pad(w2, ((0, LANES - LORA_W), (0, 0))).astype(BF16), "a0": _row(a0),
        "a2": jnp.pad(a2, ((0, LANES - LORA_A), (0, 0))).astype(BF16), "g2": g2.astype(BF16),
        "kk": _row(kk), "ka": _row(ka), "rk": _row(rk), "lnx_w": _row(lnx_w), "lnx_b": _row(lnx_b),
        "q_norm": _row(q_norm), "w_uq": w_uq_p, "kv_norm": _row(kv_norm),
        "w_uk": w_uk_t.astype(BF16),
        "wuv_pad": (wuv_all[None] * jnp.asarray(blk)).astype(BF16),
        "wuv_all": wuv_all.astype(BF16), "head_mask": jnp.asarray(hm),
        "pe_mask": jnp.asarray(pem), "seg": jnp.asarray(seg, dtype=BF16),
        "w_out": w_out.astype(BF16),
    }


def _layer1_weights(w_in, b_i, b_f, mh_norm, w_out):
    hk = C_HEADS * C_DK
    padh = lambda w: jnp.pad(w.reshape(D_MODEL, C_HEADS, C_DK),
                             ((0, 0), (0, 0), (0, HEAD_PAD - C_DK))).reshape(D_MODEL, OFF_MK)
    w_in_p = jnp.concatenate([
        padh(w_in[:, :hk]), padh(w_in[:, hk:2 * hk]), w_in[:, 2 * hk:2 * hk + 2 * MIX_O],
        _pad_cols(w_in[:, 2 * hk + 2 * MIX_O:], LANES),
    ], axis=1).astype(BF16)
    gate_bias = jnp.pad(jnp.concatenate([b_i, b_f]), (0, LANES - 2 * C_HEADS)).reshape(1, LANES)
    return {"w_in": w_in_p, "gate_bias": gate_bias.astype(F32), "mh_norm": _row(mh_norm),
            "w_out": w_out.astype(BF16)}


def _rope_tables(pos):
    inv = ROPE_BASE ** (-jnp.arange(0, QK_ROPE, 2, dtype=F32) / QK_ROPE)
    ang = pos.astype(F32)[:, None] * inv[None, :]
    return jnp.tile(jnp.cos(ang), (1, B_HEADS)), jnp.tile(jnp.sin(ang), (1, B_HEADS))


def _unpack_kpe(kpe_rot, bsz, t_len):
    return jnp.concatenate([kpe_rot[:, :ROPE_HALF], kpe_rot[:, LANES:LANES + ROPE_HALF]],
                           axis=-1).reshape(bsz, t_len, QK_ROPE)


def kernel(x_prompt, x_sample, cache_ckv, cache_kpe, page_table, state_shift, state_wkv, state_mlstm_c, state_mlstm_n, state_mlstm_m, w_in_e, mu_e, w0_e, w2_e, a0_e, a2_e, g2_e, kk_e, ka_e, rk_e, lnx_w_e, lnx_b_e, q_norm_e, w_uq_e, kv_norm_e, w_uk_e, w_uv_e, w_out_e, w_in_o, b_i_o, b_f_o, mh_norm_o, w_out_o, peer_wq, peer_k1, peer_k2, peer_u, peer_v, ln1_g, ln1_b, ln2_g, ln2_b):
    bp, tp, _ = x_prompt.shape
    bs, ts, _ = x_sample.shape
    past_len = page_table.shape[1] * PAGE_SIZE
    xp = x_prompt.reshape(bp * tp, D_MODEL)
    xs = x_sample.reshape(bs * ts, D_MODEL)
    rows_s = bs * ts
    tm_s = min(256, rows_s)
    tm_p = 256

    e = 0
    w0 = _layer0_weights(w_in_e[e], mu_e[e], w0_e[e], w2_e[e], a0_e[e], a2_e[e], g2_e[e], kk_e[e],
                         ka_e[e], rk_e[e], lnx_w_e[e], lnx_b_e[e], q_norm_e[e], w_uq_e[e],
                         kv_norm_e[e], w_uk_e[e], w_uv_e[e], w_out_e[e])
    g1, b1 = _row(ln1_g[0]), _row(ln1_b[0])
    cos_p, sin_p = _rope_tables(jnp.arange(tp))
    cos_s, sin_s = _rope_tables(past_len + jnp.arange(ts))
    cos_s = jnp.tile(cos_s, (tm_s // ts, 1))
    sin_s = jnp.tile(sin_s, (tm_s // ts, 1))

    (r, w, k, v, kk, kka, g, bonus, qcat, _, c_p, kpe_p, kcat) = _layer0_prep(
        xp, None, w0, cos_p, sin_p, bp, tp // tm_p, tm_p, tp)
    ch = lambda a: _to_chains(a, bp, tp)
    s0_p = jnp.zeros((A_HD, A_HD, bp * A_HEADS), F32)
    y, s_p = _wkv_scan(ch(r), ch(w), ch(k), ch(v), ch(kk), ch(kka), s0_p, 32)
    y = _from_chains(y, bp, tp)
    out_b = _mla_prompt(qcat, kcat, w0["wuv_pad"], bp, tp, 256)
    xp1 = _layer0_out(y, g, bonus, out_b, xp, w0, g1, b1, tm_p)
    wkv_p = jnp.transpose(s_p.reshape(A_HD, A_HD, bp, A_HEADS), (2, 3, 1, 0))

    start = _mm(state_shift[e], w0["w_in"][:, :RW_PAD], bs)
    start = jnp.repeat(start, ts, axis=0)
    (r, w, k, v, kk, kka, g, bonus, qcat, qpe, c_s, kpe_s, _) = _layer0_prep(
        xs, start, w0, cos_s, sin_s, rows_s // tm_s, 1, tm_s, ts)
    ch = lambda a: _to_chains(a, bs, ts)
    s0_s = jnp.transpose(state_wkv[e], (3, 2, 0, 1)).reshape(A_HD, A_HD, bs * A_HEADS)
    y, s_s = _wkv_scan(ch(r), ch(w), ch(k), ch(v), ch(kk), ch(kka), s0_s, ts)
    y = _from_chains(y, bs, ts)
    q_lat = qcat.reshape(bs, ts * B_HEADS, QCAT)[:, :, :KV_LORA]
    q_pe = jnp.transpose(qpe.reshape(bs, ts, 2, B_HEADS, ROPE_HALF), (0, 1, 3, 2, 4))
    q_pe = q_pe.reshape(bs, ts * B_HEADS, QK_ROPE).astype(BF16)
    kpe_new = _unpack_kpe(kpe_s, bs, ts)
    pad_new = lambda a: jnp.pad(a, ((0, 0), (0, SUBLANES - ts), (0, 0)))
    out_b = _mla_sample(q_lat, q_pe, pad_new(c_s.reshape(bs, ts, KV_LORA)), pad_new(kpe_new),
                        cache_ckv, cache_kpe, page_table, w0["wuv_all"], w0["head_mask"], e, ts)
    xs1 = _layer0_out(y, g, bonus, out_b.reshape(rows_s, B_HEADS * V_HD), xs, w0, g1, b1, tm_s)
    wkv_s = jnp.transpose(s_s.reshape(A_HD, A_HD, bs, A_HEADS), (2, 3, 1, 0))

    new_ckv_prompt = c_p.reshape(1, bp, tp, KV_LORA)
    new_kpe_prompt = _unpack_kpe(kpe_p, bp, tp)[None]
    new_ckv_sample = c_s.reshape(1, bs, ts, KV_LORA)
    new_kpe_sample = kpe_new[None]
    new_shift_prompt = x_prompt[:, -1][None]
    new_shift_sample = x_sample[:, -1][None]

    def peer_weights(l):
        return {"wq": peer_wq[l].astype(BF16), "k1": peer_k1[l].astype(BF16),
                "k2": peer_k2[l].astype(BF16), "u": peer_u[l].astype(BF16),
                "v": peer_v[l].astype(BF16)}

    pw = peer_weights(0)
    g2n, b2n = _row(ln2_g[0]), _row(ln2_b[0])
    xp2 = _peer(xp1, pw, g2n, b2n)
    xs2 = _peer(xs1, pw, g2n, b2n)

    od = 0
    w1 = _layer1_weights(w_in_o[od], b_i_o[od], b_f_o[od], mh_norm_o[od], w_out_o[od])
    g1, b1 = _row(ln1_g[1]), _row(ln1_b[1])
    pad_dk = lambda a: jnp.pad(a, [(0, 0)] * (a.ndim - 1) + [(0, HEAD_PAD - C_DK)])

    q, k, v, o_sig, gates = _layer1_prep(xp2, w1["w_in"], w1["gate_bias"], tm_p)
    zc = jnp.zeros((bp, C_HEADS, C_DV, HEAD_PAD), F32)
    zn = jnp.zeros((bp, C_HEADS, LANES), F32)
    h, c_p1, n_p1, m_p1 = _mlstm(q, k, v, gates, zc, zn, zn, bp, tp, MLSTM_CHUNK)
    xp3 = _layer1_out(h, o_sig, xp2, w1["mh_norm"], w1["w_out"], g1, b1, tm_p)

    q, k, v, o_sig, gates = _layer1_prep(xs2, w1["w_in"], w1["gate_bias"], tm_s)
    lp = MLSTM_CHUNK
    pad_t = lambda a: jnp.pad(a.reshape(bs, ts, -1), ((0, 0), (0, lp - ts), (0, 0))).reshape(
        bs * lp, -1)
    lane = jnp.arange(LANES)
    gate_fill = jnp.where(lane < C_HEADS, GATE_PAD, 0.0).astype(F32)
    gates_p = jnp.concatenate(
        [gates.reshape(bs, ts, LANES), jnp.broadcast_to(gate_fill, (bs, lp - ts, LANES))],
        axis=1).reshape(bs * lp, LANES)
    m0 = jnp.broadcast_to(state_mlstm_m[od][:, :, None], (bs, C_HEADS, LANES))
    h, c_s1, n_s1, m_s1 = _mlstm(pad_t(q), pad_t(k), pad_t(v), gates_p, pad_dk(state_mlstm_c[od]),
                                 pad_dk(state_mlstm_n[od]), m0, bs, lp, lp)
    h = h.reshape(bs, lp, MIX_O)[:, :ts].reshape(rows_s, MIX_O)
    xs3 = _layer1_out(h, o_sig, xs2, w1["mh_norm"], w1["w_out"], g1, b1, tm_s)

    pw = peer_weights(1)
    g2n, b2n = _row(ln2_g[1]), _row(ln2_b[1])
    xp4 = _peer(xp3, pw, g2n, b2n)
    xs4 = _peer(xs3, pw, g2n, b2n)

    return (xp4.reshape(bp, tp, D_MODEL), xs4.reshape(bs, ts, D_MODEL),
            new_ckv_prompt, new_kpe_prompt, new_ckv_sample, new_kpe_sample,
            new_shift_prompt, new_shift_sample, wkv_p[None], wkv_s[None],
            c_p1[..., :C_DK][None], c_s1[..., :C_DK][None],
            n_p1[..., :C_DK][None], n_s1[..., :C_DK][None],
            m_p1[..., 0][None], m_s1[..., 0][None])
```
